```python
import math
import jax
import jax.numpy as jnp
from jax import lax
import numpy as np

D_MODEL = 2048
BATCH = 1
SEQ = 16384
DEPTH = 4
DEC_BATCH = 2
DEC_SEQ = 8192
PAST_LEN = 128

HEAD_DIM = 128
A_Q_HEADS = 8
A_KV_HEADS = 2
B_Q_HEADS = 8
B_KV_HEADS = 2
A_WIDTH = A_Q_HEADS * HEAD_DIM
B_WIDTH = B_Q_HEADS * HEAD_DIM
MIX_WIDTH = A_WIDTH + B_WIDTH
A_KV_WIDTH = A_KV_HEADS * HEAD_DIM
B_KV_WIDTH = B_KV_HEADS * HEAD_DIM
IN_WIDTH = A_WIDTH + 2 * A_KV_WIDTH + B_WIDTH + 2 * B_KV_WIDTH
SPLITS = (A_WIDTH, A_WIDTH + A_KV_WIDTH, A_WIDTH + 2 * A_KV_WIDTH,
          A_WIDTH + 2 * A_KV_WIDTH + B_WIDTH, A_WIDTH + 2 * A_KV_WIDTH + B_WIDTH + B_KV_WIDTH)
D_FF = 4 * D_MODEL
Q_BLOCK = 128
WINDOW = 128
NUM_BUCKETS = 32
MAX_DISTANCE = 128
GRID_W = 64
ROPE_THETA = 10000.0
EPS = 1e-6

kernel_name = 'hymba_axial_window_sink_encoder'


def rms_norm(x, g):
    xf = x.astype(jnp.float32)
    y = xf * lax.rsqrt(jnp.mean(xf * xf, axis=-1, keepdims=True) + EPS)
    return (y * g.astype(jnp.float32)).astype(x.dtype)


def axial_rope_tables(t_len):
    rows = t_len // GRID_W
    row_ids = jnp.repeat(jnp.arange(rows, dtype=jnp.float32), GRID_W)
    col_ids = jnp.tile(jnp.arange(GRID_W, dtype=jnp.float32), rows)
    half = HEAD_DIM // 2
    inv_freq = ROPE_THETA ** (-jnp.arange(0, half, 2, dtype=jnp.float32) / half)
    ang_r = row_ids[:, None] * inv_freq[None, :]
    ang_c = col_ids[:, None] * inv_freq[None, :]
    ang = jnp.concatenate([ang_r, ang_r, ang_c, ang_c], axis=-1)
    return jnp.cos(ang), jnp.sin(ang)


def apply_axial_rope(x, cos, sin):
    xf = x.astype(jnp.float32)
    x1, x2, x3, x4 = jnp.split(xf, 4, axis=-1)
    rot = jnp.concatenate([-x2, x1, -x4, x3], axis=-1)
    return (xf * cos[None, :, None, :] + rot * sin[None, :, None, :]).astype(x.dtype)


def t5_bucket(rel):
    half = NUM_BUCKETS // 2
    max_exact = half // 2
    n = jnp.abs(rel)
    nf = jnp.maximum(n, 1).astype(jnp.float32)
    large = max_exact + (jnp.log(nf / max_exact) / math.log(MAX_DISTANCE / max_exact)
                         * (half - max_exact)).astype(jnp.int32)
    large = jnp.minimum(large, half - 1)
    return jnp.where(rel > 0, half, 0) + jnp.where(n < max_exact, n, large)


def band_bias(rel_bias):
    kw = Q_BLOCK + 2 * WINDOW
    rel = jnp.arange(kw)[None, :] - WINDOW - jnp.arange(Q_BLOCK)[:, None]
    bias = rel_bias[t5_bucket(rel)].astype(jnp.float32)
    g = B_Q_HEADS // B_KV_HEADS
    return bias.transpose(2, 0, 1).reshape(B_KV_HEADS, g, Q_BLOCK, kw), jnp.abs(rel) <= WINDOW


def global_gqa(q, k, v):
    b, t = q.shape[0], q.shape[1]
    nb = t // Q_BLOCK
    g = A_Q_HEADS // A_KV_HEADS
    qb = q.reshape(b, nb, Q_BLOCK, A_KV_HEADS, g, HEAD_DIM).transpose(1, 0, 2, 3, 4, 5)
    scale = HEAD_DIM ** -0.5

    def block(qi):
        s = jnp.einsum('bqhgd,bkhd->bhgqk', qi, k, preferred_element_type=jnp.float32) * scale
        p = jax.nn.softmax(s, axis=-1).astype(v.dtype)
        return jnp.einsum('bhgqk,bkhd->bqhgd', p, v)

    o = lax.map(block, qb)
    return o.transpose(1, 0, 2, 3, 4, 5).reshape(b, t, A_WIDTH)


def windowed_gqa_sink(q, k, v, bias, band_ok, sink):
    b, t = q.shape[0], q.shape[1]
    nb = t // Q_BLOCK
    g = B_Q_HEADS // B_KV_HEADS
    kw = Q_BLOCK + 2 * WINDOW
    n_shift = kw // Q_BLOCK
    qb = q.reshape(b, nb, Q_BLOCK, B_KV_HEADS, g, HEAD_DIM)

    def band(z):
        zp = jnp.pad(z, ((0, 0), (WINDOW, WINDOW), (0, 0), (0, 0)))
        zp = zp.reshape(b, nb + n_shift - 1, Q_BLOCK, B_KV_HEADS, HEAD_DIM)
        return jnp.concatenate([zp[:, i:i + nb] for i in range(n_shift)], axis=2)

    kb, vb = band(k), band(v)
    kpos = jnp.arange(nb)[:, None] * Q_BLOCK - WINDOW + jnp.arange(kw)[None, :]
    key_ok = (kpos >= 0) & (kpos < t)
    mask = band_ok[None, :, :] & key_ok[:, None, :]
    s = jnp.einsum('bnqhgd,bnkhd->bnhgqk', qb, kb, preferred_element_type=jnp.float32) * (HEAD_DIM ** -0.5)
    s = jnp.where(mask[None, :, None, None], s + bias[None, None], -jnp.inf)
    sk = sink.astype(jnp.float32).reshape(B_KV_HEADS, g)[None, None, :, :, None, None]
    m = jnp.maximum(jnp.max(s, axis=-1, keepdims=True), sk)
    e = jnp.exp(s - m)
    p = e / (jnp.sum(e, axis=-1, keepdims=True) + jnp.exp(sk - m))
    o = jnp.einsum('bnhgqk,bnkhd->bnqhgd', p.astype(v.dtype), vb)
    return o.reshape(b, t, B_WIDTH)


def encoder_layer(x, c, cos, sin, bias, band_ok, w_mod, b_mod, norm_attn, w_in, q_norm, k_norm,
                  sink, out_norm_a, out_norm_b, w_out, norm_mlp, w_up, w_down):
    b, t = x.shape[0], x.shape[1]
    mod = jnp.einsum('bd,de->be', jax.nn.silu(c), w_mod) + b_mod
    sh1, sc1, gt1, sh2, sc2, gt2 = jnp.split(mod[:, None, :], 6, axis=-1)

    h = rms_norm(x, norm_attn) * (1 + sc1) + sh1
    proj = jnp.einsum('btd,de->bte', h, w_in)
    qa, ka, va, qb, kb, vb = jnp.split(proj, SPLITS, axis=-1)
    qa = apply_axial_rope(rms_norm(qa.reshape(b, t, A_Q_HEADS, HEAD_DIM), q_norm), cos, sin)
    ka = apply_axial_rope(rms_norm(ka.reshape(b, t, A_KV_HEADS, HEAD_DIM), k_norm), cos, sin)
    va = va.reshape(b, t, A_KV_HEADS, HEAD_DIM)
    out_a = global_gqa(qa, ka, va)
    out_b = windowed_gqa_sink(qb.reshape(b, t, B_Q_HEADS, HEAD_DIM),
                              kb.reshape(b, t, B_KV_HEADS, HEAD_DIM),
                              vb.reshape(b, t, B_KV_HEADS, HEAD_DIM), bias, band_ok, sink)
    mix = jnp.concatenate([rms_norm(out_a, out_norm_a), rms_norm(out_b, out_norm_b)], axis=-1)
    x = x + gt1 * jnp.einsum('bte,ed->btd', mix, w_out)

    h = rms_norm(x, norm_mlp) * (1 + sc2) + sh2
    u = jax.nn.relu(jnp.einsum('btd,df->btf', h, w_up))
    x = x + gt2 * jnp.einsum('btf,fd->btd', u * u, w_down)
    return x


def run_trunk(x, c, w_mod, b_mod, norm_attn, w_in, q_norm, k_norm, sink, out_norm_a, out_norm_b,
              w_out, norm_mlp, w_up, w_down, rel_bias, norm_final):
    cos, sin = axial_rope_tables(x.shape[1])
    bias, band_ok = band_bias(rel_bias)
    for l in range(DEPTH):
        x = encoder_layer(x, c, cos, sin, bias, band_ok, w_mod[l], b_mod[l], norm_attn[l], w_in[l],
                          q_norm[l], k_norm[l], sink[l], out_norm_a[l], out_norm_b[l], w_out[l],
                          norm_mlp[l], w_up[l], w_down[l])
    return rms_norm(x, norm_final)


def setup_inputs(seed: int = 0) -> dict:
    key = jax.random.key(seed)
    ks = jax.random.split(key, 20)

    def nrm(k, shape, s):
        return jax.random.normal(k, shape, jnp.float32) * s

    return {
        'x_prompt': nrm(ks[0], (BATCH, SEQ, D_MODEL), 1.0),
        'x_sample': nrm(ks[1], (DEC_BATCH, DEC_SEQ, D_MODEL), 1.0),
        'c_prompt': nrm(ks[2], (BATCH, D_MODEL), 1.0),
        'c_sample': nrm(ks[3], (DEC_BATCH, D_MODEL), 1.0),
        'w_mod': nrm(ks[4], (DEPTH, D_MODEL, 6 * D_MODEL), D_MODEL ** -0.5),
        'b_mod': nrm(ks[5], (DEPTH, 6 * D_MODEL), 0.02),
        'norm_attn': 1.0 + nrm(ks[6], (DEPTH, D_MODEL), 0.02),
        'w_in': nrm(ks[7], (DEPTH, D_MODEL, IN_WIDTH), D_MODEL ** -0.5),
        'q_norm': 1.0 + nrm(ks[8], (DEPTH, HEAD_DIM), 0.02),
        'k_norm': 1.0 + nrm(ks[9], (DEPTH, HEAD_DIM), 0.02),
        'sink': nrm(ks[10], (DEPTH, B_Q_HEADS), 0.5),
        'out_norm_a': 1.0 + nrm(ks[11], (DEPTH, A_WIDTH), 0.02),
        'out_norm_b': 1.0 + nrm(ks[12], (DEPTH, B_WIDTH), 0.02),
        'w_out': nrm(ks[13], (DEPTH, MIX_WIDTH, D_MODEL), MIX_WIDTH ** -0.5),
        'norm_mlp': 1.0 + nrm(ks[14], (DEPTH, D_MODEL), 0.02),
        'w_up': nrm(ks[15], (DEPTH, D_MODEL, D_FF), D_MODEL ** -0.5),
        'w_down': nrm(ks[16], (DEPTH, D_FF, D_MODEL), D_FF ** -0.5),
        'rel_bias': nrm(ks[17], (NUM_BUCKETS, B_Q_HEADS), 0.5),
        'norm_final': 1.0 + nrm(ks[18], (D_MODEL,), 0.02),
    }


def reference(x_prompt, x_sample, c_prompt, c_sample, w_mod, b_mod, norm_attn, w_in, q_norm, k_norm,
              sink, out_norm_a, out_norm_b, w_out, norm_mlp, w_up, w_down, rel_bias, norm_final):
    y_prompt = run_trunk(x_prompt, c_prompt, w_mod, b_mod, norm_attn, w_in, q_norm, k_norm, sink,
                         out_norm_a, out_norm_b, w_out, norm_mlp, w_up, w_down, rel_bias, norm_final)
    y_sample = run_trunk(x_sample, c_sample, w_mod, b_mod, norm_attn, w_in, q_norm, k_norm, sink,
                         out_norm_a, out_norm_b, w_out, norm_mlp, w_up, w_down, rel_bias, norm_final)
    return (y_prompt, y_sample)
```

```python
import functools
import math

import jax
import jax.numpy as jnp
from jax import lax
from jax.experimental import pallas as pl
from jax.experimental.pallas import tpu as pltpu

D_MODEL = 2048
DEPTH = 4
HEAD_DIM = 128
A_Q_HEADS = 8
A_KV_HEADS = 2
B_Q_HEADS = 8
B_KV_HEADS = 2
GROUP = A_Q_HEADS // A_KV_HEADS
A_WIDTH = A_Q_HEADS * HEAD_DIM
B_WIDTH = B_Q_HEADS * HEAD_DIM
MIX_WIDTH = A_WIDTH + B_WIDTH
KV_WIDTH = A_KV_HEADS * HEAD_DIM
IN_WIDTH = A_WIDTH + 2 * KV_WIDTH + B_WIDTH + 2 * KV_WIDTH
D_FF = 4 * D_MODEL
Q_BLOCK = 128
WINDOW = 128
BAND = Q_BLOCK + 2 * WINDOW
NUM_BUCKETS = 32
MAX_DISTANCE = 128
GRID_W = 64
ROPE_THETA = 10000.0
EPS = 1e-6
SCALE = HEAD_DIM ** -0.5

OFF_QA = 0
OFF_KA = OFF_QA + A_WIDTH
OFF_VA = OFF_KA + KV_WIDTH
OFF_QB = OFF_VA + KV_WIDTH
OFF_KB = OFF_QB + B_WIDTH
OFF_VB = OFF_KB + KV_WIDTH

MOD_ROWS = 8
TOKEN_TILE = 512
MLP_TOKEN_TILE = 1024
FF_TILE = 512
ATTN_Q_TILE = 256
WIN_Q_TILE = 512
VMEM_LIMIT = 56 * 1024 * 1024

F32 = jnp.float32
BF16 = jnp.bfloat16


def _params(*semantics):
    return pltpu.CompilerParams(dimension_semantics=semantics, vmem_limit_bytes=VMEM_LIMIT)


def _rms(x):
    return x * lax.rsqrt(jnp.mean(x * x, axis=-1, keepdims=True) + EPS)


def _mod_kernel(c_ref, w_ref, b_ref, o_ref):
    c = c_ref[...]
    a = (c * jax.nn.sigmoid(c)).astype(BF16)
    o_ref[...] = jnp.dot(a, w_ref[...].astype(BF16), preferred_element_type=F32) + b_ref[...]


def _modulation(c_rows, w_mod, b_mod):
    tn = D_MODEL
    return pl.pallas_call(
        _mod_kernel,
        grid=(DEPTH, 6 * D_MODEL // tn),
        in_specs=[
            pl.BlockSpec((MOD_ROWS, D_MODEL), lambda l, n: (0, 0)),
            pl.BlockSpec((None, D_MODEL, tn), lambda l, n: (l, 0, n)),
            pl.BlockSpec((None, 1, tn), lambda l, n: (l, 0, n)),
        ],
        out_specs=pl.BlockSpec((None, MOD_ROWS, tn), lambda l, n: (l, 0, n)),
        out_shape=jax.ShapeDtypeStruct((DEPTH, MOD_ROWS, 6 * D_MODEL), F32),
        compiler_params=_params("arbitrary", "arbitrary"),
        name="modulation",
    )(c_rows, w_mod, b_mod.reshape(DEPTH, 1, 6 * D_MODEL))


def _bias_kernel(bucket_ref, rel_bias_ref, o_ref):
    bucket = bucket_ref[...]
    in_band = bucket >= 0
    for h in range(B_Q_HEADS):
        acc = jnp.zeros((Q_BLOCK, BAND), F32)
        for b in range(NUM_BUCKETS):
            acc = jnp.where(bucket == b, rel_bias_ref[b, h], acc)
        o_ref[h] = jnp.where(in_band, acc, -jnp.inf)


def _t5_bucket(rel):
    half = NUM_BUCKETS // 2
    max_exact = half // 2
    n = jnp.abs(rel)
    nf = jnp.maximum(n, 1).astype(F32)
    large = max_exact + (jnp.log(nf / max_exact) / math.log(MAX_DISTANCE / max_exact)
                         * (half - max_exact)).astype(jnp.int32)
    large = jnp.minimum(large, half - 1)
    return jnp.where(rel > 0, half, 0) + jnp.where(n < max_exact, n, large)


def _band_bias(rel_bias):
    rel = jnp.arange(BAND)[None, :] - WINDOW - jnp.arange(Q_BLOCK)[:, None]
    bucket = jnp.where(jnp.abs(rel) <= WINDOW, _t5_bucket(rel), -1).astype(jnp.int32)
    return pl.pallas_call(
        _bias_kernel,
        in_specs=[
            pl.BlockSpec(memory_space=pltpu.VMEM),
            pl.BlockSpec(memory_space=pltpu.SMEM),
        ],
        out_specs=pl.BlockSpec(memory_space=pltpu.VMEM),
        out_shape=jax.ShapeDtypeStruct((B_Q_HEADS, Q_BLOCK, BAND), F32),
        name="band_bias",
    )(bucket, rel_bias)


def _rope_tables(t_len):
    rows = t_len // GRID_W
    row_ids = jnp.repeat(jnp.arange(rows, dtype=F32), GRID_W)
    col_ids = jnp.tile(jnp.arange(GRID_W, dtype=F32), rows)
    half = HEAD_DIM // 2
    inv_freq = ROPE_THETA ** (-jnp.arange(0, half, 2, dtype=F32) / half)
    ang_r = row_ids[:, None] * inv_freq[None, :]
    ang_c = col_ids[:, None] * inv_freq[None, :]
    ang = jnp.concatenate([ang_r, ang_r, ang_c, ang_c], axis=-1)
    cos, sin = jnp.cos(ang), jnp.sin(ang)
    quarter = (jnp.arange(HEAD_DIM) // (HEAD_DIM // 4))[None, :]
    sin_up = jnp.where(quarter % 2 == 0, -sin, 0.0)
    sin_down = jnp.where(quarter % 2 == 1, sin, 0.0)
    return cos, sin_up, sin_down


def _inproj_kernel(x_ref, mod_ref, g_ref, w_ref, qn_ref, kn_ref, cos_ref, su_ref, sd_ref,
                   qa_ref, ka_ref, vat_ref, qb_ref, kb_ref, vb_ref):
    x = x_ref[...]
    h = (_rms(x) * g_ref[...] * (1.0 + mod_ref[1:2, :]) + mod_ref[0:1, :]).astype(BF16)

    def proj(lo, width):
        return jnp.dot(h, w_ref[:, lo:lo + width], preferred_element_type=F32)

    cos, s_up, s_down = cos_ref[...], su_ref[...], sd_ref[...]
    quarter = HEAD_DIM // 4

    def norm_rope(z, gain):
        zn = _rms(z) * gain
        return (zn * cos + pltpu.roll(zn, HEAD_DIM - quarter, 1) * s_up
                + pltpu.roll(zn, quarter, 1) * s_down)

    qa = proj(OFF_QA, A_WIDTH)
    for hd in range(A_Q_HEADS):
        sl = slice(hd * HEAD_DIM, (hd + 1) * HEAD_DIM)
        qa_ref[:, sl] = (norm_rope(qa[:, sl], qn_ref[...]) * SCALE).astype(BF16)
    ka = proj(OFF_KA, KV_WIDTH)
    for hd in range(A_KV_HEADS):
        sl = slice(hd * HEAD_DIM, (hd + 1) * HEAD_DIM)
        ka_ref[:, sl] = norm_rope(ka[:, sl], kn_ref[...]).astype(BF16)
    va = proj(OFF_VA, KV_WIDTH)
    for hd in range(A_KV_HEADS):
        sl = slice(hd * HEAD_DIM, (hd + 1) * HEAD_DIM)
        vat_ref[hd, 0] = va[:, sl].T.astype(BF16)
    qb_ref[...] = (proj(OFF_QB, B_WIDTH) * SCALE).astype(BF16)
    kb_ref[...] = proj(OFF_KB, KV_WIDTH).astype(BF16)
    vb_ref[...] = proj(OFF_VB, KV_WIDTH).astype(BF16)


def _in_projection(x, mod_rows, norm_attn, w_in, q_norm, k_norm, rope, rope_block):
    n_tok = x.shape[0]
    tm = TOKEN_TILE
    n_tiles = n_tok // tm
    per_seg = mod_rows.shape[0]
    seg_tiles = n_tiles // per_seg
    cos, s_up, s_down = rope
    row = lambda i: (i, 0)
    const = lambda i: (0, 0)
    rope_spec = pl.BlockSpec((tm, HEAD_DIM), lambda i: (rope_block(i), 0))
    return pl.pallas_call(
        _inproj_kernel,
        grid=(n_tiles,),
        in_specs=[
            pl.BlockSpec((tm, D_MODEL), row),
            pl.BlockSpec((None, 6, D_MODEL), lambda i: (i // seg_tiles, 0, 0)),
            pl.BlockSpec((1, D_MODEL), const),
            pl.BlockSpec((D_MODEL, IN_WIDTH), const),
            pl.BlockSpec((1, HEAD_DIM), const),
            pl.BlockSpec((1, HEAD_DIM), const),
            rope_spec, rope_spec, rope_spec,
        ],
        out_specs=[
            pl.BlockSpec((tm, A_WIDTH), row),
            pl.BlockSpec((tm, KV_WIDTH), row),
            pl.BlockSpec((A_KV_HEADS, 1, HEAD_DIM, tm), lambda i: (0, i, 0, 0)),
            pl.BlockSpec((tm, B_WIDTH), row),
            pl.BlockSpec((tm, KV_WIDTH), row),
            pl.BlockSpec((tm, KV_WIDTH), row),
        ],
        out_shape=[
            jax.ShapeDtypeStruct((n_tok, A_WIDTH), BF16),
            jax.ShapeDtypeStruct((n_tok, KV_WIDTH), BF16),
            jax.ShapeDtypeStruct((A_KV_HEADS, n_tiles, HEAD_DIM, tm), BF16),
            jax.ShapeDtypeStruct((n_tok, B_WIDTH), BF16),
            jax.ShapeDtypeStruct((n_tok, KV_WIDTH), BF16),
            jax.ShapeDtypeStruct((n_tok, KV_WIDTH), BF16),
        ],
        compiler_params=_params("arbitrary"),
        name="in_projection",
    )(x, mod_rows, norm_attn.reshape(1, D_MODEL), w_in, q_norm.reshape(1, HEAD_DIM),
      k_norm.reshape(1, HEAD_DIM), cos, s_up, s_down)


def _global_attn_kernel(q_ref, k_ref, vt_ref, o_ref, qt_ref, m_ref, l_ref, acc_ref, *, n_key_tiles):
    tq = q_ref.shape[0]
    tk = vt_ref.shape[-1]
    for g in range(GROUP):
        qg = q_ref[:, g * HEAD_DIM:(g + 1) * HEAD_DIM].astype(F32)
        qt_ref[:, g * tq:(g + 1) * tq] = qg.T.astype(BF16)
    m_ref[...] = jnp.full(m_ref.shape, -jnp.inf, F32)
    l_ref[...] = jnp.zeros(l_ref.shape, F32)
    acc_ref[...] = jnp.zeros(acc_ref.shape, F32)

    def body(j, carry):
        kj = k_ref[pl.ds(pl.multiple_of(j * tk, tk), tk), :]
        s = jnp.dot(kj, qt_ref[...], preferred_element_type=F32)
        m_prev = m_ref[...]
        m_new = jnp.maximum(m_prev, jnp.max(s, axis=0, keepdims=True))
        alpha = jnp.exp(m_prev - m_new)
        p = jnp.exp(s - m_new)
        l_ref[...] = alpha * l_ref[...] + jnp.sum(p, axis=0, keepdims=True)
        acc_ref[...] = alpha * acc_ref[...] + jnp.dot(vt_ref[j], p.astype(BF16),
                                                      preferred_element_type=F32)
        m_ref[...] = m_new
        return carry

    lax.fori_loop(0, n_key_tiles, body, 0)
    o = acc_ref[...] / l_ref[...]
    for g in range(GROUP):
        o_ref[:, g * HEAD_DIM:(g + 1) * HEAD_DIM] = o[:, g * tq:(g + 1) * tq].T.astype(BF16)


def _global_attention(qa, ka, vat, tok0, seq_len, n_seq):
    tq, tk = ATTN_Q_TILE, TOKEN_TILE
    q_tiles = seq_len // tq
    k_tiles = seq_len // tk
    q0 = tok0 // tq
    s0 = tok0 // seq_len
    return pl.pallas_call(
        functools.partial(_global_attn_kernel, n_key_tiles=k_tiles),
        grid=(n_seq, A_KV_HEADS, q_tiles),
        in_specs=[
            pl.BlockSpec((tq, GROUP * HEAD_DIM), lambda b, h, i: (q0 + b * q_tiles + i, h)),
            pl.BlockSpec((seq_len, HEAD_DIM), lambda b, h, i: (s0 + b, h)),
            pl.BlockSpec((None, k_tiles, HEAD_DIM, tk), lambda b, h, i: (h, s0 + b, 0, 0)),
        ],
        out_specs=pl.BlockSpec((tq, GROUP * HEAD_DIM), lambda b, h, i: (b * q_tiles + i, h)),
        out_shape=jax.ShapeDtypeStruct((n_seq * seq_len, A_WIDTH), BF16),
        scratch_shapes=[
            pltpu.VMEM((HEAD_DIM, GROUP * tq), BF16),
            pltpu.VMEM((1, GROUP * tq), F32),
            pltpu.VMEM((1, GROUP * tq), F32),
            pltpu.VMEM((HEAD_DIM, GROUP * tq), F32),
        ],
        compiler_params=_params("arbitrary", "arbitrary", "arbitrary"),
        name="global_attention",
    )(qa, ka, vat)


def _window_attn_kernel(sink_ref, q_ref, kp_ref, kc_ref, kn_ref, vp_ref, vc_ref, vn_ref, bias_ref,
                        o_ref, *, seq_starts, seq_ends):
    i = pl.program_id(0)
    sub_blocks = q_ref.shape[0] // Q_BLOCK
    col = lax.broadcasted_iota(jnp.int32, (1, BAND), 1)
    for h in range(B_KV_HEADS):
        hs = slice(h * HEAD_DIM, (h + 1) * HEAD_DIM)
        k_all = jnp.concatenate([kp_ref[:, hs], kc_ref[:, hs], kn_ref[:, hs]], axis=0)
        v_all = jnp.concatenate([vp_ref[:, hs], vc_ref[:, hs], vn_ref[:, hs]], axis=0)
        bias = bias_ref[h * GROUP:(h + 1) * GROUP].reshape(GROUP * Q_BLOCK, BAND)
        sink = jnp.concatenate(
            [jnp.full((Q_BLOCK, 1), sink_ref[h * GROUP + g], F32) for g in range(GROUP)], axis=0)
        for u in range(sub_blocks):
            blk = i * sub_blocks + u
            first = functools.reduce(jnp.logical_or, [blk == s for s in seq_starts])
            last = functools.reduce(jnp.logical_or, [blk == e - 1 for e in seq_ends])
            lo = jnp.where(first, WINDOW, 0)
            hi = jnp.where(last, WINDOW + Q_BLOCK, BAND)
            key_ok = (col >= lo) & (col < hi)
            rows = slice(u * Q_BLOCK, (u + 1) * Q_BLOCK)
            q4 = jnp.concatenate(
                [q_ref[rows, (h * GROUP + g) * HEAD_DIM:(h * GROUP + g + 1) * HEAD_DIM]
                 for g in range(GROUP)], axis=0)
            kw = k_all[u * Q_BLOCK:u * Q_BLOCK + BAND]
            vw = v_all[u * Q_BLOCK:u * Q_BLOCK + BAND]
            s = lax.dot_general(q4, kw, (((1,), (1,)), ((), ())), preferred_element_type=F32)
            s = jnp.where(key_ok, s + bias, -jnp.inf)
            m = jnp.maximum(jnp.max(s, axis=-1, keepdims=True), sink)
            e = jnp.exp(s - m)
            den = jnp.sum(e, axis=-1, keepdims=True) + jnp.exp(sink - m)
            o = jnp.dot(e.astype(BF16), vw, preferred_element_type=F32) / den
            for g in range(GROUP):
                o_ref[rows, (h * GROUP + g) * HEAD_DIM:(h * GROUP + g + 1) * HEAD_DIM] = (
                    o[g * Q_BLOCK:(g + 1) * Q_BLOCK].astype(BF16))


def _window_attention(qb, kb, vb, bias, sink, seq_bounds):
    n_tok = qb.shape[0]
    tq = WIN_Q_TILE
    sub = tq // Q_BLOCK
    n_blocks = n_tok // Q_BLOCK
    seq_starts = tuple(s // Q_BLOCK for s, _ in seq_bounds)
    seq_ends = tuple(e // Q_BLOCK for _, e in seq_bounds)
    cur = pl.BlockSpec((tq, KV_WIDTH), lambda i: (i, 0))
    prev = pl.BlockSpec((Q_BLOCK, KV_WIDTH), lambda i: (jnp.maximum(i * sub - 1, 0), 0))
    nxt = pl.BlockSpec((Q_BLOCK, KV_WIDTH), lambda i: (jnp.minimum(i * sub + sub, n_blocks - 1), 0))
    return pl.pallas_call(
        functools.partial(_window_attn_kernel, seq_starts=seq_starts, seq_ends=seq_ends),
        grid=(n_tok // tq,),
        in_specs=[
            pl.BlockSpec(memory_space=pltpu.SMEM),
            pl.BlockSpec((tq, B_WIDTH), lambda i: (i, 0)),
            prev, cur, nxt, prev, cur, nxt,
            pl.BlockSpec((B_Q_HEADS, Q_BLOCK, BAND), lambda i: (0, 0, 0)),
        ],
        out_specs=pl.BlockSpec((tq, B_WIDTH), lambda i: (i, 0)),
        out_shape=jax.ShapeDtypeStruct((n_tok, B_WIDTH), BF16),
        compiler_params=_params("arbitrary"),
        name="window_attention",
    )(sink, qb, kb, kb, kb, vb, vb, vb, bias)


def _outproj_kernel(a_ref, b_ref, x_ref, mod_ref, ga_ref, gb_ref, w_ref, o_ref):
    mix_a = (_rms(a_ref[...].astype(F32)) * ga_ref[...]).astype(BF16)
    mix_b = (_rms(b_ref[...].astype(F32)) * gb_ref[...]).astype(BF16)
    y = (jnp.dot(mix_a, w_ref[:A_WIDTH, :], preferred_element_type=F32)
         + jnp.dot(mix_b, w_ref[A_WIDTH:, :], preferred_element_type=F32))
    o_ref[...] = x_ref[...] + mod_ref[2:3, :] * y


def _out_projection(out_a, out_b, x, mod_rows, out_norm_a, out_norm_b, w_out):
    n_tok = x.shape[0]
    tm = TOKEN_TILE
    n_tiles = n_tok // tm
    seg_tiles = n_tiles // mod_rows.shape[0]
    row = lambda i: (i, 0)
    const = lambda i: (0, 0)
    return pl.pallas_call(
        _outproj_kernel,
        grid=(n_tiles,),
        in_specs=[
            pl.BlockSpec((tm, A_WIDTH), row),
            pl.BlockSpec((tm, B_WIDTH), row),
            pl.BlockSpec((tm, D_MODEL), row),
            pl.BlockSpec((None, 6, D_MODEL), lambda i: (i // seg_tiles, 0, 0)),
            pl.BlockSpec((1, A_WIDTH), const),
            pl.BlockSpec((1, B_WIDTH), const),
            pl.BlockSpec((MIX_WIDTH, D_MODEL), const),
        ],
        out_specs=pl.BlockSpec((tm, D_MODEL), row),
        out_shape=jax.ShapeDtypeStruct((n_tok, D_MODEL), F32),
        compiler_params=_params("arbitrary"),
        name="out_projection",
    )(out_a, out_b, x, mod_rows, out_norm_a.reshape(1, A_WIDTH), out_norm_b.reshape(1, B_WIDTH), w_out)


def _mlp_kernel(x_ref, mod_ref, g_ref, wu_ref, wd_ref, gf_ref, o_ref, h_ref, *, final_norm):
    k = pl.program_id(1)

    @pl.when(k == 0)
    def _():
        h_ref[...] = (_rms(x_ref[...]) * g_ref[...] * (1.0 + mod_ref[4:5, :])
                      + mod_ref[3:4, :]).astype(BF16)
        o_ref[...] = jnp.zeros(o_ref.shape, F32)

    u = jnp.maximum(jnp.dot(h_ref[...], wu_ref[...], preferred_element_type=F32), 0.0)
    o_ref[...] += jnp.dot((u * u).astype(BF16), wd_ref[...], preferred_element_type=F32)

    @pl.when(k == pl.num_programs(1) - 1)
    def _():
        y = x_ref[...] + mod_ref[5:6, :] * o_ref[...]
        if final_norm:
            y = _rms(y) * gf_ref[...]
        o_ref[...] = y


def _mlp(x, mod_rows, norm_mlp, w_up, w_down, norm_final, final_norm):
    n_tok = x.shape[0]
    tm, tf = MLP_TOKEN_TILE, FF_TILE
    tm = min(tm, n_tok // mod_rows.shape[0])
    n_tiles = n_tok // tm
    seg_tiles = n_tiles // mod_rows.shape[0]
    return pl.pallas_call(
        functools.partial(_mlp_kernel, final_norm=final_norm),
        grid=(n_tiles, D_FF // tf),
        in_specs=[
            pl.BlockSpec((tm, D_MODEL), lambda i, k: (i, 0)),
            pl.BlockSpec((None, 6, D_MODEL), lambda i, k: (i // seg_tiles, 0, 0)),
            pl.BlockSpec((1, D_MODEL), lambda i, k: (0, 0)),
            pl.BlockSpec((D_MODEL, tf), lambda i, k: (0, k)),
            pl.BlockSpec((tf, D_MODEL), lambda i, k: (k, 0)),
            pl.BlockSpec((1, D_MODEL), lambda i, k: (0, 0)),
        ],
        out_specs=pl.BlockSpec((tm, D_MODEL), lambda i, k: (i, 0)),
        out_shape=jax.ShapeDtypeStruct((n_tok, D_MODEL), F32),
        scratch_shapes=[pltpu.VMEM((tm, D_MODEL), BF16)],
        compiler_params=_params("arbitrary", "arbitrary"),
        name="mlp",
    )(x, mod_rows, norm_mlp.reshape(1, D_MODEL), w_up, w_down, norm_final.reshape(1, D_MODEL))


def kernel(x_prompt, x_sample, c_prompt, c_sample, w_mod, b_mod, norm_attn, w_in, q_norm, k_norm,
           sink, out_norm_a, out_norm_b, w_out, norm_mlp, w_up, w_down, rel_bias, norm_final):
    n_prompt, t_prompt, _ = x_prompt.shape
    n_sample, t_sample, _ = x_sample.shape
    tok_prompt = n_prompt * t_prompt
    n_tok = tok_prompt + n_sample * t_sample
    seg = math.gcd(t_prompt, t_sample)
    seg_row = ([b for b in range(n_prompt) for _ in range(t_prompt // seg)]
               + [n_prompt + b for b in range(n_sample) for _ in range(t_sample // seg)])
    seq_bounds = ([(b * t_prompt, (b + 1) * t_prompt) for b in range(n_prompt)]
                  + [(tok_prompt + b * t_sample, tok_prompt + (b + 1) * t_sample)
                     for b in range(n_sample)])

    x = jnp.concatenate([x_prompt.reshape(tok_prompt, D_MODEL),
                         x_sample.reshape(n_tok - tok_prompt, D_MODEL)], axis=0)
    c_rows = jnp.concatenate([c_prompt, c_sample], axis=0)
    c_rows = jnp.pad(c_rows, ((0, MOD_ROWS - c_rows.shape[0]), (0, 0)))
    mod = _modulation(c_rows, w_mod, b_mod)
    mod = mod.reshape(DEPTH, MOD_ROWS, 6, D_MODEL)[:, jnp.asarray(seg_row)]

    rope = _rope_tables(max(t_prompt, t_sample))
    tiles_prompt, tiles_sample = t_prompt // TOKEN_TILE, t_sample // TOKEN_TILE
    n_tiles_prompt = tok_prompt // TOKEN_TILE

    def rope_block(i):
        return jnp.where(i < n_tiles_prompt, i % tiles_prompt, (i - n_tiles_prompt) % tiles_sample)

    bias = _band_bias(rel_bias)
    w_in_b, w_out_b = w_in.astype(BF16), w_out.astype(BF16)
    w_up_b, w_down_b = w_up.astype(BF16), w_down.astype(BF16)

    for l in range(DEPTH):
        qa, ka, vat, qb, kb, vb = _in_projection(x, mod[l], norm_attn[l], w_in_b[l], q_norm[l],
                                                 k_norm[l], rope, rope_block)
        out_a = jnp.concatenate([
            _global_attention(qa, ka, vat, 0, t_prompt, n_prompt),
            _global_attention(qa, ka, vat, tok_prompt, t_sample, n_sample)], axis=0)
        out_b = _window_attention(qb, kb, vb, bias, sink[l], seq_bounds)
        x = _out_projection(out_a, out_b, x, mod[l], out_norm_a[l], out_norm_b[l], w_out_b[l])
        x = _mlp(x, mod[l], norm_mlp[l], w_up_b[l], w_down_b[l], norm_final, l == DEPTH - 1)

    y_prompt = x[:tok_prompt].reshape(n_prompt, t_prompt, D_MODEL)
    y_sample = x[tok_prompt:].reshape(n_sample, t_sample, D_MODEL)
    return y_prompt, y_sample
```

```python
import functools
import math

import jax
import jax.numpy as jnp
from jax import lax
from jax.experimental import pallas as pl
from jax.experimental.pallas import tpu as pltpu

D_MODEL = 2048
DEPTH = 4
HEAD_DIM = 128
A_Q_HEADS = 8
A_KV_HEADS = 2
B_Q_HEADS = 8
B_KV_HEADS = 2
GROUP = A_Q_HEADS // A_KV_HEADS
A_WIDTH = A_Q_HEADS * HEAD_DIM
B_WIDTH = B_Q_HEADS * HEAD_DIM
MIX_WIDTH = A_WIDTH + B_WIDTH
KV_WIDTH = A_KV_HEADS * HEAD_DIM
IN_WIDTH = A_WIDTH + 2 * KV_WIDTH + B_WIDTH + 2 * KV_WIDTH
D_FF = 4 * D_MODEL
Q_BLOCK = 128
WINDOW = 128
BAND = Q_BLOCK + 2 * WINDOW
NUM_BUCKETS = 32
MAX_DISTANCE = 128
GRID_W = 64
ROPE_THETA = 10000.0
EPS = 1e-6
SCALE = HEAD_DIM ** -0.5
LOG2E = math.log2(math.e)

OFF_QA = 0
OFF_KA = OFF_QA + A_WIDTH
OFF_VA = OFF_KA + KV_WIDTH
OFF_QB = OFF_VA + KV_WIDTH
OFF_KB = OFF_QB + B_WIDTH
OFF_VB = OFF_KB + KV_WIDTH

MOD_ROWS = 8
TOKEN_TILE = 512
MLP_TOKEN_TILE = 1024
FF_TILE = 512
ATTN_Q_TILE = 256
ATTN_CHUNKS = 4
ATTN_SLOTS = 3
SAFE_EXP2_RANGE = 60.0
SUBLANES = 8
WIN_Q_TILE = 512
VMEM_LIMIT = 56 * 1024 * 1024

F32 = jnp.float32
BF16 = jnp.bfloat16


def _params(*semantics):
    return pltpu.CompilerParams(dimension_semantics=semantics, vmem_limit_bytes=VMEM_LIMIT)


def _rms(x):
    return x * lax.rsqrt(jnp.mean(x * x, axis=-1, keepdims=True) + EPS)


def _mod_kernel(c_ref, w_ref, b_ref, o_ref):
    c = c_ref[...]
    a = (c * jax.nn.sigmoid(c)).astype(BF16)
    o_ref[...] = jnp.dot(a, w_ref[...].astype(BF16), preferred_element_type=F32) + b_ref[...]


def _modulation(c_rows, w_mod, b_mod):
    tn = D_MODEL
    return pl.pallas_call(
        _mod_kernel,
        grid=(DEPTH, 6 * D_MODEL // tn),
        in_specs=[
            pl.BlockSpec((MOD_ROWS, D_MODEL), lambda l, n: (0, 0)),
            pl.BlockSpec((None, D_MODEL, tn), lambda l, n: (l, 0, n)),
            pl.BlockSpec((None, 1, tn), lambda l, n: (l, 0, n)),
        ],
        out_specs=pl.BlockSpec((None, MOD_ROWS, tn), lambda l, n: (l, 0, n)),
        out_shape=jax.ShapeDtypeStruct((DEPTH, MOD_ROWS, 6 * D_MODEL), F32),
        compiler_params=_params("arbitrary", "arbitrary"),
        name="modulation",
    )(c_rows, w_mod, b_mod.reshape(DEPTH, 1, 6 * D_MODEL))


def _bias_kernel(bucket_ref, rel_bias_ref, o_ref):
    bucket = bucket_ref[...]
    in_band = bucket >= 0
    for h in range(B_Q_HEADS):
        acc = jnp.zeros((Q_BLOCK, BAND), F32)
        for b in range(NUM_BUCKETS):
            acc = jnp.where(bucket == b, rel_bias_ref[b, h], acc)
        o_ref[h] = jnp.where(in_band, acc, -jnp.inf)


def _t5_bucket(rel):
    half = NUM_BUCKETS // 2
    max_exact = half // 2
    n = jnp.abs(rel)
    nf = jnp.maximum(n, 1).astype(F32)
    large = max_exact + (jnp.log(nf / max_exact) / math.log(MAX_DISTANCE / max_exact)
                         * (half - max_exact)).astype(jnp.int32)
    large = jnp.minimum(large, half - 1)
    return jnp.where(rel > 0, half, 0) + jnp.where(n < max_exact, n, large)


def _band_bias(rel_bias):
    rel = jnp.arange(BAND)[None, :] - WINDOW - jnp.arange(Q_BLOCK)[:, None]
    bucket = jnp.where(jnp.abs(rel) <= WINDOW, _t5_bucket(rel), -1).astype(jnp.int32)
    return pl.pallas_call(
        _bias_kernel,
        in_specs=[
            pl.BlockSpec(memory_space=pltpu.VMEM),
            pl.BlockSpec(memory_space=pltpu.SMEM),
        ],
        out_specs=pl.BlockSpec(memory_space=pltpu.VMEM),
        out_shape=jax.ShapeDtypeStruct((B_Q_HEADS, Q_BLOCK, BAND), F32),
        name="band_bias",
    )(bucket, rel_bias)


def _rope_tables(t_len):
    rows = t_len // GRID_W
    row_ids = jnp.repeat(jnp.arange(rows, dtype=F32), GRID_W)
    col_ids = jnp.tile(jnp.arange(GRID_W, dtype=F32), rows)
    half = HEAD_DIM // 2
    inv_freq = ROPE_THETA ** (-jnp.arange(0, half, 2, dtype=F32) / half)
    ang_r = row_ids[:, None] * inv_freq[None, :]
    ang_c = col_ids[:, None] * inv_freq[None, :]
    ang = jnp.concatenate([ang_r, ang_r, ang_c, ang_c], axis=-1)
    cos, sin = jnp.cos(ang), jnp.sin(ang)
    quarter = (jnp.arange(HEAD_DIM) // (HEAD_DIM // 4))[None, :]
    sin_up = jnp.where(quarter % 2 == 0, -sin, 0.0)
    sin_down = jnp.where(quarter % 2 == 1, sin, 0.0)
    return cos, sin_up, sin_down


def _inproj_kernel(x_ref, mod_ref, g_ref, w_ref, qn_ref, kn_ref, cos_ref, su_ref, sd_ref,
                   qa_ref, ka_ref, vat_ref, qb_ref, kb_ref, vb_ref):
    x = x_ref[...]
    h = (_rms(x) * g_ref[...] * (1.0 + mod_ref[1:2, :]) + mod_ref[0:1, :]).astype(BF16)

    def proj(lo, width):
        return jnp.dot(h, w_ref[:, lo:lo + width], preferred_element_type=F32)

    cos, s_up, s_down = cos_ref[...], su_ref[...], sd_ref[...]
    quarter = HEAD_DIM // 4

    def norm_rope(z, gain):
        zn = _rms(z) * gain
        return (zn * cos + pltpu.roll(zn, HEAD_DIM - quarter, 1) * s_up
                + pltpu.roll(zn, quarter, 1) * s_down)

    qa = proj(OFF_QA, A_WIDTH)
    for hd in range(A_Q_HEADS):
        sl = slice(hd * HEAD_DIM, (hd + 1) * HEAD_DIM)
        qa_ref[:, sl] = (norm_rope(qa[:, sl], qn_ref[...]) * (SCALE * LOG2E)).astype(BF16)
    ka = proj(OFF_KA, KV_WIDTH)
    for hd in range(A_KV_HEADS):
        sl = slice(hd * HEAD_DIM, (hd + 1) * HEAD_DIM)
        ka_ref[:, sl] = norm_rope(ka[:, sl], kn_ref[...]).astype(BF16)
    va = proj(OFF_VA, KV_WIDTH)
    for hd in range(A_KV_HEADS):
        sl = slice(hd * HEAD_DIM, (hd + 1) * HEAD_DIM)
        vat_ref[hd, 0] = va[:, sl].T.astype(BF16)
    qb_ref[...] = (proj(OFF_QB, B_WIDTH) * SCALE).astype(BF16)
    kb_ref[...] = proj(OFF_KB, KV_WIDTH).astype(BF16)
    vb_ref[...] = proj(OFF_VB, KV_WIDTH).astype(BF16)


def _in_projection(x, mod_rows, norm_attn, w_in, q_norm, k_norm, rope, rope_block):
    n_tok = x.shape[0]
    tm = TOKEN_TILE
    n_tiles = n_tok // tm
    per_seg = mod_rows.shape[0]
    seg_tiles = n_tiles // per_seg
    cos, s_up, s_down = rope
    row = lambda i: (i, 0)
    const = lambda i: (0, 0)
    rope_spec = pl.BlockSpec((tm, HEAD_DIM), lambda i: (rope_block(i), 0))
    return pl.pallas_call(
        _inproj_kernel,
        grid=(n_tiles,),
        in_specs=[
            pl.BlockSpec((tm, D_MODEL), row),
            pl.BlockSpec((None, 6, D_MODEL), lambda i: (i // seg_tiles, 0, 0)),
            pl.BlockSpec((1, D_MODEL), const),
            pl.BlockSpec((D_MODEL, IN_WIDTH), const),
            pl.BlockSpec((1, HEAD_DIM), const),
            pl.BlockSpec((1, HEAD_DIM), const),
            rope_spec, rope_spec, rope_spec,
        ],
        out_specs=[
            pl.BlockSpec((tm, A_WIDTH), row),
            pl.BlockSpec((tm, KV_WIDTH), row),
            pl.BlockSpec((A_KV_HEADS, 1, HEAD_DIM, tm), lambda i: (0, i, 0, 0)),
            pl.BlockSpec((tm, B_WIDTH), row),
            pl.BlockSpec((tm, KV_WIDTH), row),
            pl.BlockSpec((tm, KV_WIDTH), row),
        ],
        out_shape=[
            jax.ShapeDtypeStruct((n_tok, A_WIDTH), BF16),
            jax.ShapeDtypeStruct((n_tok, KV_WIDTH), BF16),
            jax.ShapeDtypeStruct((A_KV_HEADS, n_tiles, HEAD_DIM, tm), BF16),
            jax.ShapeDtypeStruct((n_tok, B_WIDTH), BF16),
            jax.ShapeDtypeStruct((n_tok, KV_WIDTH), BF16),
            jax.ShapeDtypeStruct((n_tok, KV_WIDTH), BF16),
        ],
        compiler_params=_params("arbitrary"),
        name="in_projection",
    )(x, mod_rows, norm_attn.reshape(1, D_MODEL), w_in, q_norm.reshape(1, HEAD_DIM),
      k_norm.reshape(1, HEAD_DIM), cos, s_up, s_down)


def _global_attn_kernel(q_ref, k_ref, vt_ref, o_ref, qt_ref, *scratch, n_key_tiles):
    tq = q_ref.shape[0]
    tk = vt_ref.shape[-1]
    n_lanes = GROUP * tq
    rows = tk // ATTN_CHUNKS
    cols = n_lanes // ATTN_CHUNKS
    n = ATTN_SLOTS
    s_bufs, p_bufs, a_bufs, c_bufs = (scratch[i * n:(i + 1) * n] for i in range(4))
    m_ref, l_ref, acc_ref, ksq_ref = scratch[4 * n:]
    q_sq = []
    for g in range(GROUP):
        qg = q_ref[:, g * HEAD_DIM:(g + 1) * HEAD_DIM].astype(F32)
        qt_ref[:, g * tq:(g + 1) * tq] = qg.T.astype(BF16)
        q_sq.append(jnp.max(jnp.sum(qg * qg, axis=-1, keepdims=True)))

    @pl.when(pl.program_id(2) == 0)
    def _():
        def tile_max(j, best):
            kj = k_ref[pl.ds(pl.multiple_of(j * tk, tk), tk), :].astype(F32)
            return jnp.maximum(best, jnp.max(jnp.sum(kj * kj, axis=-1, keepdims=True)))

        ksq_ref[0] = lax.fori_loop(0, n_key_tiles, tile_max, jnp.float32(0.0))

    bounded = functools.reduce(jnp.maximum, q_sq) * ksq_ref[0] <= SAFE_EXP2_RANGE ** 2
    l_ref[...] = jnp.zeros(l_ref.shape, F32)
    acc_ref[...] = jnp.zeros(acc_ref.shape, F32)

    def fold(x, op):
        return functools.reduce(op, [x[r:r + SUBLANES] for r in range(0, x.shape[0], SUBLANES)])

    def rounds(first, last, one_step):
        n_rounds = (last - first) // n

        def round_(i, carry):
            for u in range(n):
                one_step(first + n * i + u, first + u)
            return carry

        lax.fori_loop(0, n_rounds, round_, 0)
        for t in range(first + n_rounds * n, last):
            one_step(t, t)

    def bounded_step(qk, pv):
        if qk is not None:
            l_part = l_ref[...]
        for c in range(ATTN_CHUNKS):
            rs = slice(c * rows, (c + 1) * rows)
            cs = slice(c * cols, (c + 1) * cols)
            if qk is not None:
                j, slot = qk
                kc = k_ref[pl.ds(pl.multiple_of(j * tk + c * rows, rows), rows), :]
                p = jnp.exp2(jnp.dot(kc, qt_ref[...], preferred_element_type=F32))
                l_part = l_part + fold(p, jnp.add)
                p_bufs[slot][rs, :] = p.astype(BF16)
            if pv is not None:
                j, slot = pv
                acc_ref[:, cs] += jnp.dot(vt_ref[j], p_bufs[slot][:, cs],
                                          preferred_element_type=F32)
        if qk is not None:
            l_ref[...] = l_part

    @pl.when(bounded)
    def _():
        bounded_step((0, 0), None)
        rounds(0, n_key_tiles - 1,
               lambda t, ts: bounded_step((t + 1, (ts + 1) % n), (t, ts % n)))
        bounded_step(None, (n_key_tiles - 1, (n_key_tiles - 1) % n))

    def step(qk, sm, pv):
        if sm is not None:
            m_prev = m_ref[...]
            m_new = jnp.maximum(m_prev, c_bufs[sm][...])
            alpha = jnp.exp2(m_prev - m_new)
            a_bufs[sm][...] = alpha
            m_ref[...] = m_new
            l_part = alpha * l_ref[...]
        if qk is not None:
            col_max = jnp.full((SUBLANES, n_lanes), -jnp.inf, F32)
        for c in range(ATTN_CHUNKS):
            rs = slice(c * rows, (c + 1) * rows)
            cs = slice(c * cols, (c + 1) * cols)
            if qk is not None:
                j, slot = qk
                kc = k_ref[pl.ds(pl.multiple_of(j * tk + c * rows, rows), rows), :]
                sc = jnp.dot(kc, qt_ref[...], preferred_element_type=F32)
                s_bufs[slot][rs, :] = sc
                col_max = jnp.maximum(col_max, fold(sc, jnp.maximum))
            if sm is not None:
                p = jnp.exp2(s_bufs[sm][rs, :] - m_new)
                l_part = l_part + fold(p, jnp.add)
                p_bufs[sm][rs, :] = p.astype(BF16)
            if pv is not None:
                j, slot = pv
                acc_ref[:, cs] = a_bufs[slot][:, cs] * acc_ref[:, cs] + jnp.dot(
                    vt_ref[j], p_bufs[slot][:, cs], preferred_element_type=F32)
        if qk is not None:
            c_bufs[qk[1]][...] = jnp.max(col_max, axis=0, keepdims=True)
        if sm is not None:
            l_ref[...] = l_part

    @pl.when(jnp.logical_not(bounded))
    def _():
        m_ref[...] = jnp.full(m_ref.shape, -jnp.inf, F32)
        step((0, 0), None, None)
        step((1, 1 % n), 0, None)
        rounds(1, n_key_tiles - 1,
               lambda t, ts: step((t + 1, (ts + 1) % n), ts % n, (t - 1, (ts - 1) % n)))
        step(None, (n_key_tiles - 1) % n, (n_key_tiles - 2, (n_key_tiles - 2) % n))
        step(None, None, (n_key_tiles - 1, (n_key_tiles - 1) % n))

    o = acc_ref[...] / jnp.sum(l_ref[...], axis=0, keepdims=True)
    for g in range(GROUP):
        o_ref[:, g * HEAD_DIM:(g + 1) * HEAD_DIM] = o[:, g * tq:(g + 1) * tq].T.astype(BF16)


def _global_attention(qa, ka, vat, tok0, seq_len, n_seq):
    tq, tk = ATTN_Q_TILE, TOKEN_TILE
    q_tiles = seq_len // tq
    k_tiles = seq_len // tk
    assert tok0 % seq_len == 0 and k_tiles >= 2
    q0 = tok0 // tq
    s0 = tok0 // seq_len
    n_lanes = GROUP * tq
    return pl.pallas_call(
        functools.partial(_global_attn_kernel, n_key_tiles=k_tiles),
        grid=(n_seq, A_KV_HEADS, q_tiles),
        in_specs=[
            pl.BlockSpec((tq, GROUP * HEAD_DIM), lambda b, h, i: (q0 + b * q_tiles + i, h)),
            pl.BlockSpec((seq_len, HEAD_DIM), lambda b, h, i: (s0 + b, h)),
            pl.BlockSpec((None, k_tiles, HEAD_DIM, tk), lambda b, h, i: (h, s0 + b, 0, 0)),
        ],
        out_specs=pl.BlockSpec((tq, GROUP * HEAD_DIM), lambda b, h, i: (b * q_tiles + i, h)),
        out_shape=jax.ShapeDtypeStruct((n_seq * seq_len, A_WIDTH), BF16),
        scratch_shapes=[
            pltpu.VMEM((HEAD_DIM, n_lanes), BF16),
            *[pltpu.VMEM((tk, n_lanes), F32)] * ATTN_SLOTS,
            *[pltpu.VMEM((tk, n_lanes), BF16)] * ATTN_SLOTS,
            *[pltpu.VMEM((1, n_lanes), F32)] * ATTN_SLOTS,
            *[pltpu.VMEM((1, n_lanes), F32)] * ATTN_SLOTS,
            pltpu.VMEM((1, n_lanes), F32),
            pltpu.VMEM((SUBLANES, n_lanes), F32),
            pltpu.VMEM((HEAD_DIM, n_lanes), F32),
            pltpu.SMEM((1,), F32),
        ],
        compiler_params=_params("arbitrary", "arbitrary", "arbitrary"),
        name="global_attention",
    )(qa, ka, vat)


def _window_attn_kernel(sink_ref, q_ref, kp_ref, kc_ref, kn_ref, vp_ref, vc_ref, vn_ref, bias_ref,
                        o_ref, *, seq_starts, seq_ends):
    i = pl.program_id(0)
    sub_blocks = q_ref.shape[0] // Q_BLOCK
    col = lax.broadcasted_iota(jnp.int32, (1, BAND), 1)
    for h in range(B_KV_HEADS):
        hs = slice(h * HEAD_DIM, (h + 1) * HEAD_DIM)
        k_all = jnp.concatenate([kp_ref[:, hs], kc_ref[:, hs], kn_ref[:, hs]], axis=0)
        v_all = jnp.concatenate([vp_ref[:, hs], vc_ref[:, hs], vn_ref[:, hs]], axis=0)
        bias = bias_ref[h * GROUP:(h + 1) * GROUP].reshape(GROUP * Q_BLOCK, BAND)
        sink = jnp.concatenate(
            [jnp.full((Q_BLOCK, 1), sink_ref[h * GROUP + g], F32) for g in range(GROUP)], axis=0)
        for u in range(sub_blocks):
            blk = i * sub_blocks + u
            first = functools.reduce(jnp.logical_or, [blk == s for s in seq_starts])
            last = functools.reduce(jnp.logical_or, [blk == e - 1 for e in seq_ends])
            lo = jnp.where(first, WINDOW, 0)
            hi = jnp.where(last, WINDOW + Q_BLOCK, BAND)
            key_ok = (col >= lo) & (col < hi)
            rows = slice(u * Q_BLOCK, (u + 1) * Q_BLOCK)
            q4 = jnp.concatenate(
                [q_ref[rows, (h * GROUP + g) * HEAD_DIM:(h * GROUP + g + 1) * HEAD_DIM]
                 for g in range(GROUP)], axis=0)
            kw = k_all[u * Q_BLOCK:u * Q_BLOCK + BAND]
            vw = v_all[u * Q_BLOCK:u * Q_BLOCK + BAND]
            s = lax.dot_general(q4, kw, (((1,), (1,)), ((), ())), preferred_element_type=F32)
            s = jnp.where(key_ok, s + bias, -jnp.inf)
            m = jnp.maximum(jnp.max(s, axis=-1, keepdims=True), sink)
            e = jnp.exp(s - m)
            den = jnp.sum(e, axis=-1, keepdims=True) + jnp.exp(sink - m)
            o = jnp.dot(e.astype(BF16), vw, preferred_element_type=F32) / den
            for g in range(GROUP):
                o_ref[rows, (h * GROUP + g) * HEAD_DIM:(h * GROUP + g + 1) * HEAD_DIM] = (
                    o[g * Q_BLOCK:(g + 1) * Q_BLOCK].astype(BF16))


def _window_attention(qb, kb, vb, bias, sink, seq_bounds):
    n_tok = qb.shape[0]
    tq = WIN_Q_TILE
    sub = tq // Q_BLOCK
    n_blocks = n_tok // Q_BLOCK
    seq_starts = tuple(s // Q_BLOCK for s, _ in seq_bounds)
    seq_ends = tuple(e // Q_BLOCK for _, e in seq_bounds)
    cur = pl.BlockSpec((tq, KV_WIDTH), lambda i: (i, 0))
    prev = pl.BlockSpec((Q_BLOCK, KV_WIDTH), lambda i: (jnp.maximum(i * sub - 1, 0), 0))
    nxt = pl.BlockSpec((Q_BLOCK, KV_WIDTH), lambda i: (jnp.minimum(i * sub + sub, n_blocks - 1), 0))
    return pl.pallas_call(
        functools.partial(_window_attn_kernel, seq_starts=seq_starts, seq_ends=seq_ends),
        grid=(n_tok // tq,),
        in_specs=[
            pl.BlockSpec(memory_space=pltpu.SMEM),
            pl.BlockSpec((tq, B_WIDTH), lambda i: (i, 0)),
            prev, cur, nxt, prev, cur, nxt,
            pl.BlockSpec((B_Q_HEADS, Q_BLOCK, BAND), lambda i: (0, 0, 0)),
        ],
        out_specs=pl.BlockSpec((tq, B_WIDTH), lambda i: (i, 0)),
        out_shape=jax.ShapeDtypeStruct((n_tok, B_WIDTH), BF16),
        compiler_params=_params("arbitrary"),
        name="window_attention",
    )(sink, qb, kb, kb, kb, vb, vb, vb, bias)


def _outproj_kernel(a_ref, b_ref, x_ref, mod_ref, ga_ref, gb_ref, w_ref, o_ref):
    mix_a = (_rms(a_ref[...].astype(F32)) * ga_ref[...]).astype(BF16)
    mix_b = (_rms(b_ref[...].astype(F32)) * gb_ref[...]).astype(BF16)
    y = (jnp.dot(mix_a, w_ref[:A_WIDTH, :], preferred_element_type=F32)
         + jnp.dot(mix_b, w_ref[A_WIDTH:, :], preferred_element_type=F32))
    o_ref[...] = x_ref[...] + mod_ref[2:3, :] * y


def _out_projection(out_a, out_b, x, mod_rows, out_norm_a, out_norm_b, w_out):
    n_tok = x.shape[0]
    tm = TOKEN_TILE
    n_tiles = n_tok // tm
    seg_tiles = n_tiles // mod_rows.shape[0]
    row = lambda i: (i, 0)
    const = lambda i: (0, 0)
    return pl.pallas_call(
        _outproj_kernel,
        grid=(n_tiles,),
        in_specs=[
            pl.BlockSpec((tm, A_WIDTH), row),
            pl.BlockSpec((tm, B_WIDTH), row),
            pl.BlockSpec((tm, D_MODEL), row),
            pl.BlockSpec((None, 6, D_MODEL), lambda i: (i // seg_tiles, 0, 0)),
            pl.BlockSpec((1, A_WIDTH), const),
            pl.BlockSpec((1, B_WIDTH), const),
            pl.BlockSpec((MIX_WIDTH, D_MODEL), const),
        ],
        out_specs=pl.BlockSpec((tm, D_MODEL), row),
        out_shape=jax.ShapeDtypeStruct((n_tok, D_MODEL), F32),
        compiler_params=_params("arbitrary"),
        name="out_projection",
    )(out_a, out_b, x, mod_rows, out_norm_a.reshape(1, A_WIDTH), out_norm_b.reshape(1, B_WIDTH), w_out)


def _mlp_kernel(x_ref, mod_ref, g_ref, wu_ref, wd_ref, gf_ref, o_ref, h_ref, *, final_norm):
    k = pl.program_id(1)

    @pl.when(k == 0)
    def _():
        h_ref[...] = (_rms(x_ref[...]) * g_ref[...] * (1.0 + mod_ref[4:5, :])
                      + mod_ref[3:4, :]).astype(BF16)
        o_ref[...] = jnp.zeros(o_ref.shape, F32)

    u = jnp.maximum(jnp.dot(h_ref[...], wu_ref[...], preferred_element_type=F32), 0.0)
    o_ref[...] += jnp.dot((u * u).astype(BF16), wd_ref[...], preferred_element_type=F32)

    @pl.when(k == pl.num_programs(1) - 1)
    def _():
        y = x_ref[...] + mod_ref[5:6, :] * o_ref[...]
        if final_norm:
            y = _rms(y) * gf_ref[...]
        o_ref[...] = y


def _mlp(x, mod_rows, norm_mlp, w_up, w_down, norm_final, final_norm):
    n_tok = x.shape[0]
    tm, tf = MLP_TOKEN_TILE, FF_TILE
    tm = min(tm, n_tok // mod_rows.shape[0])
    n_tiles = n_tok // tm
    seg_tiles = n_tiles // mod_rows.shape[0]
    return pl.pallas_call(
        functools.partial(_mlp_kernel, final_norm=final_norm),
        grid=(n_tiles, D_FF // tf),
        in_specs=[
            pl.BlockSpec((tm, D_MODEL), lambda i, k: (i, 0)),
            pl.BlockSpec((None, 6, D_MODEL), lambda i, k: (i // seg_tiles, 0, 0)),
            pl.BlockSpec((1, D_MODEL), lambda i, k: (0, 0)),
            pl.BlockSpec((D_MODEL, tf), lambda i, k: (0, k)),
            pl.BlockSpec((tf, D_MODEL), lambda i, k: (k, 0)),
            pl.BlockSpec((1, D_MODEL), lambda i, k: (0, 0)),
        ],
        out_specs=pl.BlockSpec((tm, D_MODEL), lambda i, k: (i, 0)),
        out_shape=jax.ShapeDtypeStruct((n_tok, D_MODEL), F32),
        scratch_shapes=[pltpu.VMEM((tm, D_MODEL), BF16)],
        compiler_params=_params("arbitrary", "arbitrary"),
        name="mlp",
    )(x, mod_rows, norm_mlp.reshape(1, D_MODEL), w_up, w_down, norm_final.reshape(1, D_MODEL))


def kernel(x_prompt, x_sample, c_prompt, c_sample, w_mod, b_mod, norm_attn, w_in, q_norm, k_norm,
           sink, out_norm_a, out_norm_b, w_out, norm_mlp, w_up, w_down, rel_bias, norm_final):
    n_prompt, t_prompt, _ = x_prompt.shape
    n_sample, t_sample, _ = x_sample.shape
    tok_prompt = n_prompt * t_prompt
    n_tok = tok_prompt + n_sample * t_sample
    seg = math.gcd(t_prompt, t_sample)
    seg_row = ([b for b in range(n_prompt) for _ in range(t_prompt // seg)]
               + [n_prompt + b for b in range(n_sample) for _ in range(t_sample // seg)])
    seq_bounds = ([(b * t_prompt, (b + 1) * t_prompt) for b in range(n_prompt)]
                  + [(tok_prompt + b * t_sample, tok_prompt + (b + 1) * t_sample)
                     for b in range(n_sample)])

    x = jnp.concatenate([x_prompt.reshape(tok_prompt, D_MODEL),
                         x_sample.reshape(n_tok - tok_prompt, D_MODEL)], axis=0)
    c_rows = jnp.concatenate([c_prompt, c_sample], axis=0)
    c_rows = jnp.pad(c_rows, ((0, MOD_ROWS - c_rows.shape[0]), (0, 0)))
    mod = _modulation(c_rows, w_mod, b_mod)
    mod = mod.reshape(DEPTH, MOD_ROWS, 6, D_MODEL)[:, jnp.asarray(seg_row)]

    rope = _rope_tables(max(t_prompt, t_sample))
    tiles_prompt, tiles_sample = t_prompt // TOKEN_TILE, t_sample // TOKEN_TILE
    n_tiles_prompt = tok_prompt // TOKEN_TILE

    def rope_block(i):
        return jnp.where(i < n_tiles_prompt, i % tiles_prompt, (i - n_tiles_prompt) % tiles_sample)

    bias = _band_bias(rel_bias)
    w_in_b, w_out_b = w_in.astype(BF16), w_out.astype(BF16)
    w_up_b, w_down_b = w_up.astype(BF16), w_down.astype(BF16)

    for l in range(DEPTH):
        qa, ka, vat, qb, kb, vb = _in_projection(x, mod[l], norm_attn[l], w_in_b[l], q_norm[l],
                                                 k_norm[l], rope, rope_block)
        out_a = jnp.concatenate([
            _global_attention(qa, ka, vat, 0, t_prompt, n_prompt),
            _global_attention(qa, ka, vat, tok_prompt, t_sample, n_sample)], axis=0)
        out_b = _window_attention(qb, kb, vb, bias, sink[l], seq_bounds)
        x = _out_projection(out_a, out_b, x, mod[l], out_norm_a[l], out_norm_b[l], w_out_b[l])
        x = _mlp(x, mod[l], norm_mlp[l], w_up_b[l], w_down_b[l], norm_final, l == DEPTH - 1)

    y_prompt = x[:tok_prompt].reshape(n_prompt, t_prompt, D_MODEL)
    y_sample = x[tok_prompt:].reshape(n_sample, t_sample, D_MODEL)
    return y_prompt, y_sample
```

```python
import functools
import math

import jax
import jax.numpy as jnp
from jax import lax
from jax.experimental import pallas as pl
from jax.experimental.pallas import tpu as pltpu

D_MODEL = 2048
DEPTH = 4
HEAD_DIM = 128
A_Q_HEADS = 8
A_KV_HEADS = 2
B_Q_HEADS = 8
B_KV_HEADS = 2
GROUP = A_Q_HEADS // A_KV_HEADS
A_WIDTH = A_Q_HEADS * HEAD_DIM
B_WIDTH = B_Q_HEADS * HEAD_DIM
MIX_WIDTH = A_WIDTH + B_WIDTH
KV_WIDTH = A_KV_HEADS * HEAD_DIM
IN_WIDTH = A_WIDTH + 2 * KV_WIDTH + B_WIDTH + 2 * KV_WIDTH
D_FF = 4 * D_MODEL
Q_BLOCK = 128
WINDOW = 128
BAND = Q_BLOCK + 2 * WINDOW
NUM_BUCKETS = 32
MAX_DISTANCE = 128
GRID_W = 64
ROPE_THETA = 10000.0
EPS = 1e-6
SCALE = HEAD_DIM ** -0.5
LOG2E = math.log2(math.e)

OFF_QA = 0
OFF_KA = OFF_QA + A_WIDTH
OFF_VA = OFF_KA + KV_WIDTH
OFF_QB = OFF_VA + KV_WIDTH
OFF_KB = OFF_QB + B_WIDTH
OFF_VB = OFF_KB + KV_WIDTH

MOD_ROWS = 8
TOKEN_TILE = 512
MLP_TOKEN_TILE = 1024
FF_TILE = 512
ATTN_Q_TILE = 256
ATTN_CHUNKS = 4
ATTN_SLOTS = 3
SAFE_EXP2_RANGE = 60.0
SUBLANES = 8
WIN_Q_TILE = 512
VMEM_LIMIT = 60 * 1024 * 1024

F32 = jnp.float32
BF16 = jnp.bfloat16


def _params(*semantics):
    return pltpu.CompilerParams(dimension_semantics=semantics, vmem_limit_bytes=VMEM_LIMIT)


def _rms(x):
    return x * lax.rsqrt(jnp.mean(x * x, axis=-1, keepdims=True) + EPS)


def _mod_kernel(c_ref, w_ref, b_ref, o_ref):
    c = c_ref[...]
    a = (c * jax.nn.sigmoid(c)).astype(BF16)
    o_ref[...] = jnp.dot(a, w_ref[...].astype(BF16), preferred_element_type=F32) + b_ref[...]


def _modulation(c_rows, w_mod, b_mod):
    tn = D_MODEL
    return pl.pallas_call(
        _mod_kernel,
        grid=(DEPTH, 6 * D_MODEL // tn),
        in_specs=[
            pl.BlockSpec((MOD_ROWS, D_MODEL), lambda l, n: (0, 0)),
            pl.BlockSpec((None, D_MODEL, tn), lambda l, n: (l, 0, n)),
            pl.BlockSpec((None, 1, tn), lambda l, n: (l, 0, n)),
        ],
        out_specs=pl.BlockSpec((None, MOD_ROWS, tn), lambda l, n: (l, 0, n)),
        out_shape=jax.ShapeDtypeStruct((DEPTH, MOD_ROWS, 6 * D_MODEL), F32),
        compiler_params=_params("arbitrary", "arbitrary"),
        name="modulation",
    )(c_rows, w_mod, b_mod.reshape(DEPTH, 1, 6 * D_MODEL))


def _bias_kernel(bucket_ref, rel_bias_ref, o_ref):
    bucket = bucket_ref[...]
    in_band = bucket >= 0
    for h in range(B_Q_HEADS):
        acc = jnp.zeros((BAND, Q_BLOCK), F32)
        for b in range(NUM_BUCKETS):
            acc = jnp.where(bucket == b, rel_bias_ref[b, h], acc)
        g = h % GROUP
        o_ref[h // GROUP, :, g * Q_BLOCK:(g + 1) * Q_BLOCK] = jnp.where(in_band, acc * LOG2E, -jnp.inf)


def _t5_bucket(rel):
    half = NUM_BUCKETS // 2
    max_exact = half // 2
    n = jnp.abs(rel)
    nf = jnp.maximum(n, 1).astype(F32)
    large = max_exact + (jnp.log(nf / max_exact) / math.log(MAX_DISTANCE / max_exact)
                         * (half - max_exact)).astype(jnp.int32)
    large = jnp.minimum(large, half - 1)
    return jnp.where(rel > 0, half, 0) + jnp.where(n < max_exact, n, large)


def _band_bias(rel_bias):
    rel = jnp.arange(BAND)[:, None] - WINDOW - jnp.arange(Q_BLOCK)[None, :]
    bucket = jnp.where(jnp.abs(rel) <= WINDOW, _t5_bucket(rel), -1).astype(jnp.int32)
    return pl.pallas_call(
        _bias_kernel,
        in_specs=[
            pl.BlockSpec(memory_space=pltpu.VMEM),
            pl.BlockSpec(memory_space=pltpu.SMEM),
        ],
        out_specs=pl.BlockSpec(memory_space=pltpu.VMEM),
        out_shape=jax.ShapeDtypeStruct((B_KV_HEADS, BAND, GROUP * Q_BLOCK), F32),
        name="band_bias",
    )(bucket, rel_bias)


def _rope_tables(t_len):
    rows = t_len // GRID_W
    row_ids = jnp.repeat(jnp.arange(rows, dtype=F32), GRID_W)
    col_ids = jnp.tile(jnp.arange(GRID_W, dtype=F32), rows)
    half = HEAD_DIM // 2
    inv_freq = ROPE_THETA ** (-jnp.arange(0, half, 2, dtype=F32) / half)
    ang_r = row_ids[:, None] * inv_freq[None, :]
    ang_c = col_ids[:, None] * inv_freq[None, :]
    ang = jnp.concatenate([ang_r, ang_r, ang_c, ang_c], axis=-1)
    cos, sin = jnp.cos(ang), jnp.sin(ang)
    quarter = (jnp.arange(HEAD_DIM) // (HEAD_DIM // 4))[None, :]
    sin_up = jnp.where(quarter % 2 == 0, -sin, 0.0)
    sin_down = jnp.where(quarter % 2 == 1, sin, 0.0)
    return cos, sin_up, sin_down


def _inproj_kernel(x_ref, mod_ref, g_ref, w_ref, qn_ref, kn_ref, cos_ref, su_ref, sd_ref,
                   qat_ref, ka_ref, vat_ref, qbt_ref, kb_ref, vbt_ref):
    x = x_ref[...]
    h = (_rms(x) * g_ref[...] * (1.0 + mod_ref[1:2, :]) + mod_ref[0:1, :]).astype(BF16)

    def proj(lo, width):
        return jnp.dot(h, w_ref[:, lo:lo + width], preferred_element_type=F32)

    cos, s_up, s_down = cos_ref[...], su_ref[...], sd_ref[...]
    quarter = HEAD_DIM // 4

    def norm_rope(z, gain):
        zn = _rms(z) * gain
        return (zn * cos + pltpu.roll(zn, HEAD_DIM - quarter, 1) * s_up
                + pltpu.roll(zn, quarter, 1) * s_down)

    qa = proj(OFF_QA, A_WIDTH)
    for hd in range(A_Q_HEADS):
        sl = slice(hd * HEAD_DIM, (hd + 1) * HEAD_DIM)
        qat_ref[hd] = (norm_rope(qa[:, sl], qn_ref[...]) * (SCALE * LOG2E)).T.astype(BF16)
    ka = proj(OFF_KA, KV_WIDTH)
    for hd in range(A_KV_HEADS):
        sl = slice(hd * HEAD_DIM, (hd + 1) * HEAD_DIM)
        ka_ref[:, sl] = norm_rope(ka[:, sl], kn_ref[...]).astype(BF16)
    va = proj(OFF_VA, KV_WIDTH)
    for hd in range(A_KV_HEADS):
        sl = slice(hd * HEAD_DIM, (hd + 1) * HEAD_DIM)
        vat_ref[hd, 0] = va[:, sl].T.astype(BF16)
    qb = proj(OFF_QB, B_WIDTH)
    for hd in range(B_Q_HEADS):
        sl = slice(hd * HEAD_DIM, (hd + 1) * HEAD_DIM)
        qbt_ref[hd] = (qb[:, sl] * (SCALE * LOG2E)).T.astype(BF16)
    kb_ref[...] = proj(OFF_KB, KV_WIDTH).astype(BF16)
    vb = proj(OFF_VB, KV_WIDTH)
    for hd in range(B_KV_HEADS):
        sl = slice(hd * HEAD_DIM, (hd + 1) * HEAD_DIM)
        vbt_ref[hd] = vb[:, sl].T.astype(BF16)


def _in_projection(x, mod_rows, norm_attn, w_in, q_norm, k_norm, rope, rope_block):
    n_tok = x.shape[0]
    tm = TOKEN_TILE
    n_tiles = n_tok // tm
    per_seg = mod_rows.shape[0]
    seg_tiles = n_tiles // per_seg
    cos, s_up, s_down = rope
    row = lambda i: (i, 0)
    const = lambda i: (0, 0)
    rope_spec = pl.BlockSpec((tm, HEAD_DIM), lambda i: (rope_block(i), 0))
    return pl.pallas_call(
        _inproj_kernel,
        grid=(n_tiles,),
        in_specs=[
            pl.BlockSpec((tm, D_MODEL), row),
            pl.BlockSpec((None, 6, D_MODEL), lambda i: (i // seg_tiles, 0, 0)),
            pl.BlockSpec((1, D_MODEL), const),
            pl.BlockSpec((D_MODEL, IN_WIDTH), const),
            pl.BlockSpec((1, HEAD_DIM), const),
            pl.BlockSpec((1, HEAD_DIM), const),
            rope_spec, rope_spec, rope_spec,
        ],
        out_specs=[
            pl.BlockSpec((A_Q_HEADS, HEAD_DIM, tm), lambda i: (0, 0, i)),
            pl.BlockSpec((tm, KV_WIDTH), row),
            pl.BlockSpec((A_KV_HEADS, 1, HEAD_DIM, tm), lambda i: (0, i, 0, 0)),
            pl.BlockSpec((B_Q_HEADS, HEAD_DIM, tm), lambda i: (0, 0, i)),
            pl.BlockSpec((tm, KV_WIDTH), row),
            pl.BlockSpec((B_KV_HEADS, HEAD_DIM, tm), lambda i: (0, 0, i)),
        ],
        out_shape=[
            jax.ShapeDtypeStruct((A_Q_HEADS, HEAD_DIM, n_tok), BF16),
            jax.ShapeDtypeStruct((n_tok, KV_WIDTH), BF16),
            jax.ShapeDtypeStruct((A_KV_HEADS, n_tiles, HEAD_DIM, tm), BF16),
            jax.ShapeDtypeStruct((B_Q_HEADS, HEAD_DIM, n_tok), BF16),
            jax.ShapeDtypeStruct((n_tok, KV_WIDTH), BF16),
            jax.ShapeDtypeStruct((B_KV_HEADS, HEAD_DIM, n_tok), BF16),
        ],
        compiler_params=_params("arbitrary"),
        name="in_projection",
    )(x, mod_rows, norm_attn.reshape(1, D_MODEL), w_in, q_norm.reshape(1, HEAD_DIM),
      k_norm.reshape(1, HEAD_DIM), cos, s_up, s_down)


def _global_attn_kernel(q_ref, k_ref, vt_ref, o_ref, qt_ref, *scratch, n_key_tiles):
    tq = q_ref.shape[-1]
    tk = vt_ref.shape[-1]
    n_lanes = GROUP * tq
    rows = tk // ATTN_CHUNKS
    cols = n_lanes // ATTN_CHUNKS
    n = ATTN_SLOTS
    s_bufs, p_bufs, a_bufs, c_bufs = (scratch[i * n:(i + 1) * n] for i in range(4))
    m_ref, l_ref, acc_ref, ksq_ref = scratch[4 * n:]
    q_sq = []
    for g in range(GROUP):
        qt_ref[:, g * tq:(g + 1) * tq] = q_ref[g]
        qg = q_ref[g].astype(F32)
        q_sq.append(jnp.max(jnp.sum(qg * qg, axis=0, keepdims=True)))

    @pl.when(pl.program_id(2) == 0)
    def _():
        def tile_max(j, best):
            kj = k_ref[pl.ds(pl.multiple_of(j * tk, tk), tk), :].astype(F32)
            return jnp.maximum(best, jnp.max(jnp.sum(kj * kj, axis=-1, keepdims=True)))

        ksq_ref[0] = lax.fori_loop(0, n_key_tiles, tile_max, jnp.float32(0.0))

    bounded = functools.reduce(jnp.maximum, q_sq) * ksq_ref[0] <= SAFE_EXP2_RANGE ** 2
    l_ref[...] = jnp.zeros(l_ref.shape, F32)
    acc_ref[...] = jnp.zeros(acc_ref.shape, F32)

    def fold(x, op):
        return functools.reduce(op, [x[r:r + SUBLANES] for r in range(0, x.shape[0], SUBLANES)])

    def rounds(first, last, one_step):
        n_rounds = (last - first) // n

        def round_(i, carry):
            for u in range(n):
                one_step(first + n * i + u, first + u)
            return carry

        lax.fori_loop(0, n_rounds, round_, 0)
        for t in range(first + n_rounds * n, last):
            one_step(t, t)

    def bounded_step(qk, pv):
        if qk is not None:
            l_part = l_ref[...]
        for c in range(ATTN_CHUNKS):
            rs = slice(c * rows, (c + 1) * rows)
            cs = slice(c * cols, (c + 1) * cols)
            if qk is not None:
                j, slot = qk
                kc = k_ref[pl.ds(pl.multiple_of(j * tk + c * rows, rows), rows), :]
                p = jnp.exp2(jnp.dot(kc, qt_ref[...], preferred_element_type=F32))
                l_part = l_part + fold(p, jnp.add)
                p_bufs[slot][rs, :] = p.astype(BF16)
            if pv is not None:
                j, slot = pv
                acc_ref[:, cs] += jnp.dot(vt_ref[j], p_bufs[slot][:, cs],
                                          preferred_element_type=F32)
        if qk is not None:
            l_ref[...] = l_part

    @pl.when(bounded)
    def _():
        bounded_step((0, 0), None)
        rounds(0, n_key_tiles - 1,
               lambda t, ts: bounded_step((t + 1, (ts + 1) % n), (t, ts % n)))
        bounded_step(None, (n_key_tiles - 1, (n_key_tiles - 1) % n))

    def step(qk, sm, pv):
        if sm is not None:
            m_prev = m_ref[...]
            m_new = jnp.maximum(m_prev, c_bufs[sm][...])
            alpha = jnp.exp2(m_prev - m_new)
            a_bufs[sm][...] = alpha
            m_ref[...] = m_new
            l_part = alpha * l_ref[...]
        if qk is not None:
            col_max = jnp.full((SUBLANES, n_lanes), -jnp.inf, F32)
        for c in range(ATTN_CHUNKS):
            rs = slice(c * rows, (c + 1) * rows)
            cs = slice(c * cols, (c + 1) * cols)
            if qk is not None:
                j, slot = qk
                kc = k_ref[pl.ds(pl.multiple_of(j * tk + c * rows, rows), rows), :]
                sc = jnp.dot(kc, qt_ref[...], preferred_element_type=F32)
                s_bufs[slot][rs, :] = sc
                col_max = jnp.maximum(col_max, fold(sc, jnp.maximum))
            if sm is not None:
                p = jnp.exp2(s_bufs[sm][rs, :] - m_new)
                l_part = l_part + fold(p, jnp.add)
                p_bufs[sm][rs, :] = p.astype(BF16)
            if pv is not None:
                j, slot = pv
                acc_ref[:, cs] = a_bufs[slot][:, cs] * acc_ref[:, cs] + jnp.dot(
                    vt_ref[j], p_bufs[slot][:, cs], preferred_element_type=F32)
        if qk is not None:
            c_bufs[qk[1]][...] = jnp.max(col_max, axis=0, keepdims=True)
        if sm is not None:
            l_ref[...] = l_part

    @pl.when(jnp.logical_not(bounded))
    def _():
        m_ref[...] = jnp.full(m_ref.shape, -jnp.inf, F32)
        step((0, 0), None, None)
        step((1, 1 % n), 0, None)
        rounds(1, n_key_tiles - 1,
               lambda t, ts: step((t + 1, (ts + 1) % n), ts % n, (t - 1, (ts - 1) % n)))
        step(None, (n_key_tiles - 1) % n, (n_key_tiles - 2, (n_key_tiles - 2) % n))
        step(None, None, (n_key_tiles - 1, (n_key_tiles - 1) % n))

    o = acc_ref[...] / jnp.sum(l_ref[...], axis=0, keepdims=True)
    for g in range(GROUP):
        o_ref[:, g * HEAD_DIM:(g + 1) * HEAD_DIM] = o[:, g * tq:(g + 1) * tq].T.astype(BF16)


def _global_attention(qa, ka, vat, tok0, seq_len, n_seq):
    tq, tk = ATTN_Q_TILE, TOKEN_TILE
    q_tiles = seq_len // tq
    k_tiles = seq_len // tk
    assert tok0 % seq_len == 0 and k_tiles >= 2
    q0 = tok0 // tq
    s0 = tok0 // seq_len
    n_lanes = GROUP * tq
    return pl.pallas_call(
        functools.partial(_global_attn_kernel, n_key_tiles=k_tiles),
        grid=(n_seq, A_KV_HEADS, q_tiles),
        in_specs=[
            pl.BlockSpec((GROUP, HEAD_DIM, tq), lambda b, h, i: (h, 0, q0 + b * q_tiles + i)),
            pl.BlockSpec((seq_len, HEAD_DIM), lambda b, h, i: (s0 + b, h)),
            pl.BlockSpec((None, k_tiles, HEAD_DIM, tk), lambda b, h, i: (h, s0 + b, 0, 0)),
        ],
        out_specs=pl.BlockSpec((tq, GROUP * HEAD_DIM), lambda b, h, i: (b * q_tiles + i, h)),
        out_shape=jax.ShapeDtypeStruct((n_seq * seq_len, A_WIDTH), BF16),
        scratch_shapes=[
            pltpu.VMEM((HEAD_DIM, n_lanes), BF16),
            *[pltpu.VMEM((tk, n_lanes), F32)] * ATTN_SLOTS,
            *[pltpu.VMEM((tk, n_lanes), BF16)] * ATTN_SLOTS,
            *[pltpu.VMEM((1, n_lanes), F32)] * ATTN_SLOTS,
            *[pltpu.VMEM((1, n_lanes), F32)] * ATTN_SLOTS,
            pltpu.VMEM((1, n_lanes), F32),
            pltpu.VMEM((SUBLANES, n_lanes), F32),
            pltpu.VMEM((HEAD_DIM, n_lanes), F32),
            pltpu.SMEM((1,), F32),
        ],
        compiler_params=_params("arbitrary", "arbitrary", "arbitrary"),
        name="global_attention",
    )(qa, ka, vat)


def _window_attn_kernel(sink_ref, qt_ref, kp_ref, kc_ref, kn_ref, vtp_ref, vtc_ref, vtn_ref, bias_ref,
                        o_ref, *, seq_starts, seq_ends):
    i = pl.program_id(0)
    sub_blocks = qt_ref.shape[-1] // Q_BLOCK
    n_lanes = GROUP * Q_BLOCK
    lane_group = lax.broadcasted_iota(jnp.int32, (1, n_lanes), 1) // Q_BLOCK
    neg_inf = jnp.full((WINDOW, n_lanes), -jnp.inf, F32)

    def fold(x, op):
        return functools.reduce(op, [x[r:r + SUBLANES] for r in range(0, x.shape[0], SUBLANES)])

    for h in range(B_KV_HEADS):
        hs = slice(h * HEAD_DIM, (h + 1) * HEAD_DIM)
        k_all = jnp.concatenate([kp_ref[:, hs], kc_ref[:, hs], kn_ref[:, hs]], axis=0)
        vt_all = jnp.concatenate([vtp_ref[h], vtc_ref[h], vtn_ref[h]], axis=1)
        sink = jnp.full((1, n_lanes), sink_ref[h * GROUP] * LOG2E, F32)
        for g in range(1, GROUP):
            sink = jnp.where(lane_group == g, sink_ref[h * GROUP + g] * LOG2E, sink)
        for u in range(sub_blocks):
            blk = i * sub_blocks + u
            first = functools.reduce(jnp.logical_or, [blk == s for s in seq_starts])
            last = functools.reduce(jnp.logical_or, [blk == e - 1 for e in seq_ends])
            qs = slice(u * Q_BLOCK, (u + 1) * Q_BLOCK)
            qt = jnp.concatenate([qt_ref[h * GROUP + g, :, qs] for g in range(GROUP)], axis=1)
            kw = k_all[u * Q_BLOCK:u * Q_BLOCK + BAND]
            s = jnp.dot(kw, qt, preferred_element_type=F32) + bias_ref[h]
            s_prev = jnp.where(first, neg_inf, s[:WINDOW])
            s_mid = s[WINDOW:WINDOW + Q_BLOCK]
            s_next = jnp.where(last, neg_inf, s[WINDOW + Q_BLOCK:])
            col_max = functools.reduce(
                jnp.maximum, [fold(x, jnp.maximum) for x in (s_prev, s_mid, s_next)])
            m = jnp.maximum(jnp.max(col_max, axis=0, keepdims=True), sink)
            e = [jnp.exp2(x - m) for x in (s_prev, s_mid, s_next)]
            col_sum = functools.reduce(jnp.add, [fold(x, jnp.add) for x in e])
            den = jnp.sum(col_sum, axis=0, keepdims=True) + jnp.exp2(sink - m)
            p = jnp.concatenate([x.astype(BF16) for x in e], axis=0)
            ot = jnp.dot(vt_all[:, u * Q_BLOCK:u * Q_BLOCK + BAND], p,
                         preferred_element_type=F32) / den
            for g in range(GROUP):
                o_ref[qs, (h * GROUP + g) * HEAD_DIM:(h * GROUP + g + 1) * HEAD_DIM] = (
                    ot[:, g * Q_BLOCK:(g + 1) * Q_BLOCK].T.astype(BF16))


def _window_attention(qbt, kb, vbt, bias, sink, seq_bounds):
    n_tok = kb.shape[0]
    tq = WIN_Q_TILE
    sub = tq // Q_BLOCK
    n_blocks = n_tok // Q_BLOCK
    seq_starts = tuple(s // Q_BLOCK for s, _ in seq_bounds)
    seq_ends = tuple(e // Q_BLOCK for _, e in seq_bounds)
    prev_block = lambda i: jnp.maximum(i * sub - 1, 0)
    next_block = lambda i: jnp.minimum(i * sub + sub, n_blocks - 1)
    k_cur = pl.BlockSpec((tq, KV_WIDTH), lambda i: (i, 0))
    k_prev = pl.BlockSpec((Q_BLOCK, KV_WIDTH), lambda i: (prev_block(i), 0))
    k_next = pl.BlockSpec((Q_BLOCK, KV_WIDTH), lambda i: (next_block(i), 0))
    vt_cur = pl.BlockSpec((B_KV_HEADS, HEAD_DIM, tq), lambda i: (0, 0, i))
    vt_prev = pl.BlockSpec((B_KV_HEADS, HEAD_DIM, Q_BLOCK), lambda i: (0, 0, prev_block(i)))
    vt_next = pl.BlockSpec((B_KV_HEADS, HEAD_DIM, Q_BLOCK), lambda i: (0, 0, next_block(i)))
    return pl.pallas_call(
        functools.partial(_window_attn_kernel, seq_starts=seq_starts, seq_ends=seq_ends),
        grid=(n_tok // tq,),
        in_specs=[
            pl.BlockSpec(memory_space=pltpu.SMEM),
            pl.BlockSpec((B_Q_HEADS, HEAD_DIM, tq), lambda i: (0, 0, i)),
            k_prev, k_cur, k_next, vt_prev, vt_cur, vt_next,
            pl.BlockSpec((B_KV_HEADS, BAND, GROUP * Q_BLOCK), lambda i: (0, 0, 0)),
        ],
        out_specs=pl.BlockSpec((tq, B_WIDTH), lambda i: (i, 0)),
        out_shape=jax.ShapeDtypeStruct((n_tok, B_WIDTH), BF16),
        compiler_params=_params("arbitrary"),
        name="window_attention",
    )(sink, qbt, kb, kb, kb, vbt, vbt, vbt, bias)


def _outproj_kernel(ap_ref, as_ref, b_ref, x_ref, mod_ref, ga_ref, gb_ref, w_ref, gm_ref,
                    o_ref, h_ref, *, prompt_tiles):
    out_a = jnp.where(pl.program_id(0) < prompt_tiles, ap_ref[...], as_ref[...])
    mix_a = (_rms(out_a.astype(F32)) * ga_ref[...]).astype(BF16)
    mix_b = (_rms(b_ref[...].astype(F32)) * gb_ref[...]).astype(BF16)
    y = (jnp.dot(mix_a, w_ref[:A_WIDTH, :], preferred_element_type=F32)
         + jnp.dot(mix_b, w_ref[A_WIDTH:, :], preferred_element_type=F32))
    x = x_ref[...] + mod_ref[2:3, :] * y
    o_ref[...] = x
    h_ref[...] = (_rms(x) * gm_ref[...] * (1.0 + mod_ref[4:5, :]) + mod_ref[3:4, :]).astype(BF16)


def _out_projection(out_a_prompt, out_a_sample, out_b, x, mod_rows, out_norm_a, out_norm_b, w_out,
                    norm_mlp):
    n_tok = x.shape[0]
    tm = TOKEN_TILE
    n_tiles = n_tok // tm
    prompt_tiles = out_a_prompt.shape[0] // tm
    seg_tiles = n_tiles // mod_rows.shape[0]
    row = lambda i: (i, 0)
    const = lambda i: (0, 0)
    return pl.pallas_call(
        functools.partial(_outproj_kernel, prompt_tiles=prompt_tiles),
        grid=(n_tiles,),
        in_specs=[
            pl.BlockSpec((tm, A_WIDTH), lambda i: (jnp.minimum(i, prompt_tiles - 1), 0)),
            pl.BlockSpec((tm, A_WIDTH), lambda i: (jnp.maximum(i - prompt_tiles, 0), 0)),
            pl.BlockSpec((tm, B_WIDTH), row),
            pl.BlockSpec((tm, D_MODEL), row),
            pl.BlockSpec((None, 6, D_MODEL), lambda i: (i // seg_tiles, 0, 0)),
            pl.BlockSpec((1, A_WIDTH), const),
            pl.BlockSpec((1, B_WIDTH), const),
            pl.BlockSpec((MIX_WIDTH, D_MODEL), const),
            pl.BlockSpec((1, D_MODEL), const),
        ],
        out_specs=[pl.BlockSpec((tm, D_MODEL), row), pl.BlockSpec((tm, D_MODEL), row)],
        out_shape=[jax.ShapeDtypeStruct((n_tok, D_MODEL), F32),
                   jax.ShapeDtypeStruct((n_tok, D_MODEL), BF16)],
        compiler_params=_params("arbitrary"),
        name="out_projection",
    )(out_a_prompt, out_a_sample, out_b, x, mod_rows, out_norm_a.reshape(1, A_WIDTH),
      out_norm_b.reshape(1, B_WIDTH), w_out, norm_mlp.reshape(1, D_MODEL))


def _mlp_kernel(x_ref, h_ref, mod_ref, wu_ref, wd_ref, gf_ref, o_ref, *, final_norm):
    k = pl.program_id(1)

    @pl.when(k == 0)
    def _():
        o_ref[...] = jnp.zeros(o_ref.shape, F32)

    u = jnp.maximum(jnp.dot(h_ref[...], wu_ref[...], preferred_element_type=F32), 0.0)
    o_ref[...] += jnp.dot((u * u).astype(BF16), wd_ref[...], preferred_element_type=F32)

    @pl.when(k == pl.num_programs(1) - 1)
    def _():
        y = x_ref[...] + mod_ref[5:6, :] * o_ref[...]
        if final_norm:
            y = _rms(y) * gf_ref[...]
        o_ref[...] = y


def _mlp(x, h, mod_rows, w_up, w_down, norm_final, final_norm):
    n_tok = x.shape[0]
    tm, tf = MLP_TOKEN_TILE, FF_TILE
    tm = min(tm, n_tok // mod_rows.shape[0])
    n_tiles = n_tok // tm
    seg_tiles = n_tiles // mod_rows.shape[0]
    return pl.pallas_call(
        functools.partial(_mlp_kernel, final_norm=final_norm),
        grid=(n_tiles, D_FF // tf),
        in_specs=[
            pl.BlockSpec((tm, D_MODEL), lambda i, k: (i, 0)),
            pl.BlockSpec((tm, D_MODEL), lambda i, k: (i, 0)),
            pl.BlockSpec((None, 6, D_MODEL), lambda i, k: (i // seg_tiles, 0, 0)),
            pl.BlockSpec((D_MODEL, tf), lambda i, k: (0, k)),
            pl.BlockSpec((tf, D_MODEL), lambda i, k: (k, 0)),
            pl.BlockSpec((1, D_MODEL), lambda i, k: (0, 0)),
        ],
        out_specs=pl.BlockSpec((tm, D_MODEL), lambda i, k: (i, 0)),
        out_shape=jax.ShapeDtypeStruct((n_tok, D_MODEL), F32),
        compiler_params=_params("arbitrary", "arbitrary"),
        name="mlp",
    )(x, h, mod_rows, w_up, w_down, norm_final.reshape(1, D_MODEL))


def kernel(x_prompt, x_sample, c_prompt, c_sample, w_mod, b_mod, norm_attn, w_in, q_norm, k_norm,
           sink, out_norm_a, out_norm_b, w_out, norm_mlp, w_up, w_down, rel_bias, norm_final):
    n_prompt, t_prompt, _ = x_prompt.shape
    n_sample, t_sample, _ = x_sample.shape
    tok_prompt = n_prompt * t_prompt
    n_tok = tok_prompt + n_sample * t_sample
    seg = math.gcd(t_prompt, t_sample)
    seg_row = ([b for b in range(n_prompt) for _ in range(t_prompt // seg)]
               + [n_prompt + b for b in range(n_sample) for _ in range(t_sample // seg)])
    seq_bounds = ([(b * t_prompt, (b + 1) * t_prompt) for b in range(n_prompt)]
                  + [(tok_prompt + b * t_sample, tok_prompt + (b + 1) * t_sample)
                     for b in range(n_sample)])

    x = jnp.concatenate([x_prompt.reshape(tok_prompt, D_MODEL),
                         x_sample.reshape(n_tok - tok_prompt, D_MODEL)], axis=0)
    c_rows = jnp.concatenate([c_prompt, c_sample], axis=0)
    c_rows = jnp.pad(c_rows, ((0, MOD_ROWS - c_rows.shape[0]), (0, 0)))
    mod = _modulation(c_rows, w_mod, b_mod)
    mod = mod.reshape(DEPTH, MOD_ROWS, 6, D_MODEL)[:, jnp.asarray(seg_row)]

    rope = _rope_tables(max(t_prompt, t_sample))
    tiles_prompt, tiles_sample = t_prompt // TOKEN_TILE, t_sample // TOKEN_TILE
    n_tiles_prompt = tok_prompt // TOKEN_TILE

    def rope_block(i):
        return jnp.where(i < n_tiles_prompt, i % tiles_prompt, (i - n_tiles_prompt) % tiles_sample)

    bias = _band_bias(rel_bias)
    w_in_b, w_out_b = w_in.astype(BF16), w_out.astype(BF16)
    w_up_b, w_down_b = w_up.astype(BF16), w_down.astype(BF16)

    for l in range(DEPTH):
        qat, ka, vat, qbt, kb, vbt = _in_projection(x, mod[l], norm_attn[l], w_in_b[l], q_norm[l],
                                                    k_norm[l], rope, rope_block)
        out_a_prompt = _global_attention(qat, ka, vat, 0, t_prompt, n_prompt)
        out_a_sample = _global_attention(qat, ka, vat, tok_prompt, t_sample, n_sample)
        out_b = _window_attention(qbt, kb, vbt, bias, sink[l], seq_bounds)
        x, h = _out_projection(out_a_prompt, out_a_sample, out_b, x, mod[l], out_norm_a[l],
                               out_norm_b[l], w_out_b[l], norm_mlp[l])
        x = _mlp(x, h, mod[l], w_up_b[l], w_down_b[l], norm_final, l == DEPTH - 1)

    y_prompt = x[:tok_prompt].reshape(n_prompt, t_prompt, D_MODEL)
    y_sample = x[tok_prompt:].reshape(n_sample, t_sample, D_MODEL)
    return y_prompt, y_sample
```

```python
import functools
import math

import jax
import jax.numpy as jnp
from jax import lax
from jax.experimental import pallas as pl
from jax.experimental.pallas import tpu as pltpu

D_MODEL = 2048
DEPTH = 4
HEAD_DIM = 128
A_Q_HEADS = 8
A_KV_HEADS = 2
B_Q_HEADS = 8
B_KV_HEADS = 2
GROUP = A_Q_HEADS // A_KV_HEADS
A_WIDTH = A_Q_HEADS * HEAD_DIM
B_WIDTH = B_Q_HEADS * HEAD_DIM
MIX_WIDTH = A_WIDTH + B_WIDTH
KV_WIDTH = A_KV_HEADS * HEAD_DIM
IN_WIDTH = A_WIDTH + 2 * KV_WIDTH + B_WIDTH + 2 * KV_WIDTH
D_FF = 4 * D_MODEL
Q_BLOCK = 128
WINDOW = 128
BAND = Q_BLOCK + 2 * WINDOW
NUM_BUCKETS = 32
MAX_DISTANCE = 128
GRID_W = 64
ROPE_THETA = 10000.0
EPS = 1e-6
SCALE = HEAD_DIM ** -0.5
LOG2E = math.log2(math.e)

OFF_QA = 0
OFF_KA = OFF_QA + A_WIDTH
OFF_VA = OFF_KA + KV_WIDTH
OFF_QB = OFF_VA + KV_WIDTH
OFF_KB = OFF_QB + B_WIDTH
OFF_VB = OFF_KB + KV_WIDTH

MOD_ROWS = 8
TOKEN_TILE = 512
MLP_TOKEN_TILE = 1024
FF_TILE = 512
ATTN_Q_TILE = 256
ATTN_CHUNKS = 1
ATTN_SLOTS = 3
SAFE_EXP2_RANGE = 60.0
SUBLANES = 8
WIN_Q_TILE = 512
VMEM_LIMIT = 60 * 1024 * 1024

F32 = jnp.float32
BF16 = jnp.bfloat16


def _params(*semantics):
    return pltpu.CompilerParams(dimension_semantics=semantics, vmem_limit_bytes=VMEM_LIMIT)


def _rms(x):
    return x * lax.rsqrt(jnp.mean(x * x, axis=-1, keepdims=True) + EPS)


def _part_starts(parts, tm):
    starts, start = [], 0
    for p in parts:
        starts.append(start)
        start += p.shape[0] // tm
    return tuple(starts)


def _part_specs(parts, tm, width):
    specs = []
    for p, start in zip(parts, _part_starts(parts, tm)):
        n = p.shape[0] // tm
        specs.append(pl.BlockSpec((tm, width),
                                  lambda i, start=start, n=n: (jnp.clip(i - start, 0, n - 1), 0)))
    return specs


def _read_parts(refs, starts):
    i = pl.program_id(0)
    x = refs[0][...]
    for ref, start in zip(refs[1:], starts[1:]):
        x = jnp.where(i >= start, ref[...], x)
    return x


def _mod_kernel(c_ref, w_ref, b_ref, o_ref):
    c = c_ref[...]
    a = (c * jax.nn.sigmoid(c)).astype(BF16)
    o_ref[...] = jnp.dot(a, w_ref[...].astype(BF16), preferred_element_type=F32) + b_ref[...]


def _modulation(c_rows, w_mod, b_mod):
    tn = D_MODEL
    return pl.pallas_call(
        _mod_kernel,
        grid=(DEPTH, 6 * D_MODEL // tn),
        in_specs=[
            pl.BlockSpec((MOD_ROWS, D_MODEL), lambda l, n: (0, 0)),
            pl.BlockSpec((None, D_MODEL, tn), lambda l, n: (l, 0, n)),
            pl.BlockSpec((None, 1, tn), lambda l, n: (l, 0, n)),
        ],
        out_specs=pl.BlockSpec((None, MOD_ROWS, tn), lambda l, n: (l, 0, n)),
        out_shape=jax.ShapeDtypeStruct((DEPTH, MOD_ROWS, 6 * D_MODEL), F32),
        compiler_params=_params("arbitrary", "arbitrary"),
        name="modulation",
    )(c_rows, w_mod, b_mod.reshape(DEPTH, 1, 6 * D_MODEL))


def _bias_kernel(bucket_ref, rel_bias_ref, o_ref):
    bucket = bucket_ref[...]
    in_band = bucket >= 0
    for h in range(B_Q_HEADS):
        acc = jnp.zeros((BAND, Q_BLOCK), F32)
        for b in range(NUM_BUCKETS):
            acc = jnp.where(bucket == b, rel_bias_ref[b, h], acc)
        g = h % GROUP
        o_ref[h // GROUP, :, g * Q_BLOCK:(g + 1) * Q_BLOCK] = jnp.where(in_band, acc * LOG2E, -jnp.inf)


def _t5_bucket(rel):
    half = NUM_BUCKETS // 2
    max_exact = half // 2
    n = jnp.abs(rel)
    nf = jnp.maximum(n, 1).astype(F32)
    large = max_exact + (jnp.log(nf / max_exact) / math.log(MAX_DISTANCE / max_exact)
                         * (half - max_exact)).astype(jnp.int32)
    large = jnp.minimum(large, half - 1)
    return jnp.where(rel > 0, half, 0) + jnp.where(n < max_exact, n, large)


def _band_bias(rel_bias):
    rel = jnp.arange(BAND)[:, None] - WINDOW - jnp.arange(Q_BLOCK)[None, :]
    bucket = jnp.where(jnp.abs(rel) <= WINDOW, _t5_bucket(rel), -1).astype(jnp.int32)
    return pl.pallas_call(
        _bias_kernel,
        in_specs=[
            pl.BlockSpec(memory_space=pltpu.VMEM),
            pl.BlockSpec(memory_space=pltpu.SMEM),
        ],
        out_specs=pl.BlockSpec(memory_space=pltpu.VMEM),
        out_shape=jax.ShapeDtypeStruct((B_KV_HEADS, BAND, GROUP * Q_BLOCK), F32),
        name="band_bias",
    )(bucket, rel_bias)


def _rope_tables(t_len):
    rows = t_len // GRID_W
    row_ids = jnp.repeat(jnp.arange(rows, dtype=F32), GRID_W)
    col_ids = jnp.tile(jnp.arange(GRID_W, dtype=F32), rows)
    half = HEAD_DIM // 2
    inv_freq = ROPE_THETA ** (-jnp.arange(0, half, 2, dtype=F32) / half)
    ang_r = row_ids[:, None] * inv_freq[None, :]
    ang_c = col_ids[:, None] * inv_freq[None, :]
    ang = jnp.concatenate([ang_r, ang_r, ang_c, ang_c], axis=-1)
    cos, sin = jnp.cos(ang), jnp.sin(ang)
    quarter = (jnp.arange(HEAD_DIM) // (HEAD_DIM // 4))[None, :]
    sin_up = jnp.where(quarter % 2 == 0, -sin, 0.0)
    sin_down = jnp.where(quarter % 2 == 1, sin, 0.0)
    return cos, sin_up, sin_down


def _inproj_kernel(*refs, x_starts):
    x_refs, refs = refs[:len(x_starts)], refs[len(x_starts):]
    (mod_ref, g_ref, w_ref, qn_ref, kn_ref, cos_ref, su_ref, sd_ref,
     qat_ref, ka_ref, vat_ref, qbt_ref, kb_ref, vbt_ref) = refs
    x = _read_parts(x_refs, x_starts)
    h = (_rms(x) * g_ref[...] * (1.0 + mod_ref[1:2, :]) + mod_ref[0:1, :]).astype(BF16)

    def proj(lo, width):
        return jnp.dot(h, w_ref[:, lo:lo + width], preferred_element_type=F32)

    cos, s_up, s_down = cos_ref[...], su_ref[...], sd_ref[...]
    quarter = HEAD_DIM // 4

    def norm_rope(z, gain):
        zn = _rms(z) * gain
        return (zn * cos + pltpu.roll(zn, HEAD_DIM - quarter, 1) * s_up
                + pltpu.roll(zn, quarter, 1) * s_down)

    qa = proj(OFF_QA, A_WIDTH)
    for hd in range(A_Q_HEADS):
        sl = slice(hd * HEAD_DIM, (hd + 1) * HEAD_DIM)
        qat_ref[hd] = (norm_rope(qa[:, sl], qn_ref[...]) * (SCALE * LOG2E)).T.astype(BF16)
    ka = proj(OFF_KA, KV_WIDTH)
    for hd in range(A_KV_HEADS):
        sl = slice(hd * HEAD_DIM, (hd + 1) * HEAD_DIM)
        ka_ref[:, sl] = norm_rope(ka[:, sl], kn_ref[...]).astype(BF16)
    va = proj(OFF_VA, KV_WIDTH)
    for hd in range(A_KV_HEADS):
        sl = slice(hd * HEAD_DIM, (hd + 1) * HEAD_DIM)
        vat_ref[hd, 0] = va[:, sl].T.astype(BF16)
    qb = proj(OFF_QB, B_WIDTH)
    for hd in range(B_Q_HEADS):
        sl = slice(hd * HEAD_DIM, (hd + 1) * HEAD_DIM)
        qbt_ref[hd] = (qb[:, sl] * (SCALE * LOG2E)).T.astype(BF16)
    kb_ref[...] = proj(OFF_KB, KV_WIDTH).astype(BF16)
    vb = proj(OFF_VB, KV_WIDTH)
    for hd in range(B_KV_HEADS):
        sl = slice(hd * HEAD_DIM, (hd + 1) * HEAD_DIM)
        vbt_ref[hd] = vb[:, sl].T.astype(BF16)


def _in_projection(x_parts, mod_rows, norm_attn, w_in, q_norm, k_norm, rope, rope_block):
    n_tok = sum(p.shape[0] for p in x_parts)
    tm = TOKEN_TILE
    n_tiles = n_tok // tm
    per_seg = mod_rows.shape[0]
    seg_tiles = n_tiles // per_seg
    cos, s_up, s_down = rope
    row = lambda i: (i, 0)
    const = lambda i: (0, 0)
    rope_spec = pl.BlockSpec((tm, HEAD_DIM), lambda i: (rope_block(i), 0))
    return pl.pallas_call(
        functools.partial(_inproj_kernel, x_starts=_part_starts(x_parts, tm)),
        grid=(n_tiles,),
        in_specs=[
            *_part_specs(x_parts, tm, D_MODEL),
            pl.BlockSpec((None, 6, D_MODEL), lambda i: (i // seg_tiles, 0, 0)),
            pl.BlockSpec((1, D_MODEL), const),
            pl.BlockSpec((D_MODEL, IN_WIDTH), const),
            pl.BlockSpec((1, HEAD_DIM), const),
            pl.BlockSpec((1, HEAD_DIM), const),
            rope_spec, rope_spec, rope_spec,
        ],
        out_specs=[
            pl.BlockSpec((A_Q_HEADS, HEAD_DIM, tm), lambda i: (0, 0, i)),
            pl.BlockSpec((tm, KV_WIDTH), row),
            pl.BlockSpec((A_KV_HEADS, 1, HEAD_DIM, tm), lambda i: (0, i, 0, 0)),
            pl.BlockSpec((B_Q_HEADS, HEAD_DIM, tm), lambda i: (0, 0, i)),
            pl.BlockSpec((tm, KV_WIDTH), row),
            pl.BlockSpec((B_KV_HEADS, HEAD_DIM, tm), lambda i: (0, 0, i)),
        ],
        out_shape=[
            jax.ShapeDtypeStruct((A_Q_HEADS, HEAD_DIM, n_tok), BF16),
            jax.ShapeDtypeStruct((n_tok, KV_WIDTH), BF16),
            jax.ShapeDtypeStruct((A_KV_HEADS, n_tiles, HEAD_DIM, tm), BF16),
            jax.ShapeDtypeStruct((B_Q_HEADS, HEAD_DIM, n_tok), BF16),
            jax.ShapeDtypeStruct((n_tok, KV_WIDTH), BF16),
            jax.ShapeDtypeStruct((B_KV_HEADS, HEAD_DIM, n_tok), BF16),
        ],
        compiler_params=_params("arbitrary"),
        name="in_projection",
    )(*x_parts, mod_rows, norm_attn.reshape(1, D_MODEL), w_in, q_norm.reshape(1, HEAD_DIM),
      k_norm.reshape(1, HEAD_DIM), cos, s_up, s_down)


def _global_attn_kernel(q_ref, k_ref, vt_ref, o_ref, qt_ref, *scratch, n_key_tiles):
    tq = q_ref.shape[-1]
    tk = vt_ref.shape[-1]
    n_lanes = GROUP * tq
    rows = tk // ATTN_CHUNKS
    cols = n_lanes // ATTN_CHUNKS
    n = ATTN_SLOTS
    s_bufs, p_bufs, a_bufs, c_bufs = (scratch[i * n:(i + 1) * n] for i in range(4))
    m_ref, l_ref, acc_ref, ksq_ref = scratch[4 * n:]
    q_sq = []
    for g in range(GROUP):
        qt_ref[:, g * tq:(g + 1) * tq] = q_ref[g]
        qg = q_ref[g].astype(F32)
        q_sq.append(jnp.max(jnp.sum(qg * qg, axis=0, keepdims=True)))

    @pl.when(pl.program_id(2) == 0)
    def _():
        def tile_max(j, best):
            kj = k_ref[pl.ds(pl.multiple_of(j * tk, tk), tk), :].astype(F32)
            return jnp.maximum(best, jnp.max(jnp.sum(kj * kj, axis=-1, keepdims=True)))

        ksq_ref[0] = lax.fori_loop(0, n_key_tiles, tile_max, jnp.float32(0.0))

    bounded = functools.reduce(jnp.maximum, q_sq) * ksq_ref[0] <= SAFE_EXP2_RANGE ** 2
    l_ref[...] = jnp.zeros(l_ref.shape, F32)
    acc_ref[...] = jnp.zeros(acc_ref.shape, F32)

    def fold(x, op):
        return functools.reduce(op, [x[r:r + SUBLANES] for r in range(0, x.shape[0], SUBLANES)])

    def rounds(first, last, one_step):
        n_rounds = (last - first) // n

        def round_(i, carry):
            for u in range(n):
                one_step(first + n * i + u, first + u)
            return carry

        lax.fori_loop(0, n_rounds, round_, 0)
        for t in range(first + n_rounds * n, last):
            one_step(t, t)

    def bounded_step(qk, pv):
        if qk is not None:
            l_part = l_ref[...]
        for c in range(ATTN_CHUNKS):
            rs = slice(c * rows, (c + 1) * rows)
            cs = slice(c * cols, (c + 1) * cols)
            if qk is not None:
                j, slot = qk
                kc = k_ref[pl.ds(pl.multiple_of(j * tk + c * rows, rows), rows), :]
                p = jnp.exp2(jnp.dot(kc, qt_ref[...], preferred_element_type=F32))
                l_part = l_part + fold(p, jnp.add)
                p_bufs[slot][rs, :] = p.astype(BF16)
            if pv is not None:
                j, slot = pv
                acc_ref[:, cs] += jnp.dot(vt_ref[j], p_bufs[slot][:, cs],
                                          preferred_element_type=F32)
        if qk is not None:
            l_ref[...] = l_part

    @pl.when(bounded)
    def _():
        bounded_step((0, 0), None)
        rounds(0, n_key_tiles - 1,
               lambda t, ts: bounded_step((t + 1, (ts + 1) % n), (t, ts % n)))
        bounded_step(None, (n_key_tiles - 1, (n_key_tiles - 1) % n))

    def step(qk, sm, pv):
        if sm is not None:
            m_prev = m_ref[...]
            m_new = jnp.maximum(m_prev, c_bufs[sm][...])
            alpha = jnp.exp2(m_prev - m_new)
            a_bufs[sm][...] = alpha
            m_ref[...] = m_new
            l_part = alpha * l_ref[...]
        if qk is not None:
            col_max = jnp.full((SUBLANES, n_lanes), -jnp.inf, F32)
        for c in range(ATTN_CHUNKS):
            rs = slice(c * rows, (c + 1) * rows)
            cs = slice(c * cols, (c + 1) * cols)
            if qk is not None:
                j, slot = qk
                kc = k_ref[pl.ds(pl.multiple_of(j * tk + c * rows, rows), rows), :]
                sc = jnp.dot(kc, qt_ref[...], preferred_element_type=F32)
                s_bufs[slot][rs, :] = sc
                col_max = jnp.maximum(col_max, fold(sc, jnp.maximum))
            if sm is not None:
                p = jnp.exp2(s_bufs[sm][rs, :] - m_new)
                l_part = l_part + fold(p, jnp.add)
                p_bufs[sm][rs, :] = p.astype(BF16)
            if pv is not None:
                j, slot = pv
                acc_ref[:, cs] = a_bufs[slot][:, cs] * acc_ref[:, cs] + jnp.dot(
                    vt_ref[j], p_bufs[slot][:, cs], preferred_element_type=F32)
        if qk is not None:
            c_bufs[qk[1]][...] = jnp.max(col_max, axis=0, keepdims=True)
        if sm is not None:
            l_ref[...] = l_part

    @pl.when(jnp.logical_not(bounded))
    def _():
        m_ref[...] = jnp.full(m_ref.shape, -jnp.inf, F32)
        step((0, 0), None, None)
        step((1, 1 % n), 0, None)
        rounds(1, n_key_tiles - 1,
               lambda t, ts: step((t + 1, (ts + 1) % n), ts % n, (t - 1, (ts - 1) % n)))
        step(None, (n_key_tiles - 1) % n, (n_key_tiles - 2, (n_key_tiles - 2) % n))
        step(None, None, (n_key_tiles - 1, (n_key_tiles - 1) % n))

    o = acc_ref[...] / jnp.sum(l_ref[...], axis=0, keepdims=True)
    for g in range(GROUP):
        o_ref[:, g * HEAD_DIM:(g + 1) * HEAD_DIM] = o[:, g * tq:(g + 1) * tq].T.astype(BF16)


def _global_attention(qat, ka, vat, tok0, seq_len, n_seq):
    tq, tk = ATTN_Q_TILE, TOKEN_TILE
    q_tiles = seq_len // tq
    k_tiles = seq_len // tk
    assert tok0 % seq_len == 0 and k_tiles >= 2
    q0 = tok0 // tq
    s0 = tok0 // seq_len
    n_lanes = GROUP * tq
    return pl.pallas_call(
        functools.partial(_global_attn_kernel, n_key_tiles=k_tiles),
        grid=(n_seq, A_KV_HEADS, q_tiles),
        in_specs=[
            pl.BlockSpec((GROUP, HEAD_DIM, tq), lambda b, h, i: (h, 0, q0 + b * q_tiles + i)),
            pl.BlockSpec((seq_len, HEAD_DIM), lambda b, h, i: (s0 + b, h)),
            pl.BlockSpec((None, k_tiles, HEAD_DIM, tk), lambda b, h, i: (h, s0 + b, 0, 0)),
        ],
        out_specs=pl.BlockSpec((tq, GROUP * HEAD_DIM), lambda b, h, i: (b * q_tiles + i, h)),
        out_shape=jax.ShapeDtypeStruct((n_seq * seq_len, A_WIDTH), BF16),
        scratch_shapes=[
            pltpu.VMEM((HEAD_DIM, n_lanes), BF16),
            *[pltpu.VMEM((tk, n_lanes), F32)] * ATTN_SLOTS,
            *[pltpu.VMEM((tk, n_lanes), BF16)] * ATTN_SLOTS,
            *[pltpu.VMEM((1, n_lanes), F32)] * ATTN_SLOTS,
            *[pltpu.VMEM((1, n_lanes), F32)] * ATTN_SLOTS,
            pltpu.VMEM((1, n_lanes), F32),
            pltpu.VMEM((SUBLANES, n_lanes), F32),
            pltpu.VMEM((HEAD_DIM, n_lanes), F32),
            pltpu.SMEM((1,), F32),
        ],
        compiler_params=_params("arbitrary", "arbitrary", "arbitrary"),
        name="global_attention",
    )(qat, ka, vat)


def _window_attn_kernel(sink_ref, qt_ref, kp_ref, kc_ref, kn_ref, vtp_ref, vtc_ref, vtn_ref, bias_ref,
                        o_ref, *, seq_starts, seq_ends):
    i = pl.program_id(0)
    sub_blocks = qt_ref.shape[-1] // Q_BLOCK
    n_lanes = GROUP * Q_BLOCK
    lane_group = lax.broadcasted_iota(jnp.int32, (1, n_lanes), 1) // Q_BLOCK
    neg_inf = jnp.full((WINDOW, n_lanes), -jnp.inf, F32)

    def fold(x, op):
        return functools.reduce(op, [x[r:r + SUBLANES] for r in range(0, x.shape[0], SUBLANES)])

    for h in range(B_KV_HEADS):
        hs = slice(h * HEAD_DIM, (h + 1) * HEAD_DIM)
        k_all = jnp.concatenate([kp_ref[:, hs], kc_ref[:, hs], kn_ref[:, hs]], axis=0)
        vt_all = jnp.concatenate([vtp_ref[h], vtc_ref[h], vtn_ref[h]], axis=1)
        sink = jnp.full((1, n_lanes), sink_ref[h * GROUP] * LOG2E, F32)
        for g in range(1, GROUP):
            sink = jnp.where(lane_group == g, sink_ref[h * GROUP + g] * LOG2E, sink)
        for u in range(sub_blocks):
            blk = i * sub_blocks + u
            first = functools.reduce(jnp.logical_or, [blk == s for s in seq_starts])
            last = functools.reduce(jnp.logical_or, [blk == e - 1 for e in seq_ends])
            qs = slice(u * Q_BLOCK, (u + 1) * Q_BLOCK)
            qt = jnp.concatenate([qt_ref[h * GROUP + g, :, qs] for g in range(GROUP)], axis=1)
            kw = k_all[u * Q_BLOCK:u * Q_BLOCK + BAND]
            s = jnp.dot(kw, qt, preferred_element_type=F32) + bias_ref[h]
            s_prev = jnp.where(first, neg_inf, s[:WINDOW])
            s_mid = s[WINDOW:WINDOW + Q_BLOCK]
            s_next = jnp.where(last, neg_inf, s[WINDOW + Q_BLOCK:])
            col_max = functools.reduce(
                jnp.maximum, [fold(x, jnp.maximum) for x in (s_prev, s_mid, s_next)])
            m = jnp.maximum(jnp.max(col_max, axis=0, keepdims=True), sink)
            e = [jnp.exp2(x - m) for x in (s_prev, s_mid, s_next)]
            col_sum = functools.reduce(jnp.add, [fold(x, jnp.add) for x in e])
            den = jnp.sum(col_sum, axis=0, keepdims=True) + jnp.exp2(sink - m)
            p = jnp.concatenate([x.astype(BF16) for x in e], axis=0)
            ot = jnp.dot(vt_all[:, u * Q_BLOCK:u * Q_BLOCK + BAND], p,
                         preferred_element_type=F32) / den
            for g in range(GROUP):
                o_ref[qs, (h * GROUP + g) * HEAD_DIM:(h * GROUP + g + 1) * HEAD_DIM] = (
                    ot[:, g * Q_BLOCK:(g + 1) * Q_BLOCK].T.astype(BF16))


def _window_attention(qbt, kb, vbt, bias, sink, seq_bounds):
    n_tok = kb.shape[0]
    tq = WIN_Q_TILE
    sub = tq // Q_BLOCK
    n_blocks = n_tok // Q_BLOCK
    seq_starts = tuple(s // Q_BLOCK for s, _ in seq_bounds)
    seq_ends = tuple(e // Q_BLOCK for _, e in seq_bounds)
    prev_block = lambda i: jnp.maximum(i * sub - 1, 0)
    next_block = lambda i: jnp.minimum(i * sub + sub, n_blocks - 1)
    k_cur = pl.BlockSpec((tq, KV_WIDTH), lambda i: (i, 0))
    k_prev = pl.BlockSpec((Q_BLOCK, KV_WIDTH), lambda i: (prev_block(i), 0))
    k_next = pl.BlockSpec((Q_BLOCK, KV_WIDTH), lambda i: (next_block(i), 0))
    vt_cur = pl.BlockSpec((B_KV_HEADS, HEAD_DIM, tq), lambda i: (0, 0, i))
    vt_prev = pl.BlockSpec((B_KV_HEADS, HEAD_DIM, Q_BLOCK), lambda i: (0, 0, prev_block(i)))
    vt_next = pl.BlockSpec((B_KV_HEADS, HEAD_DIM, Q_BLOCK), lambda i: (0, 0, next_block(i)))
    return pl.pallas_call(
        functools.partial(_window_attn_kernel, seq_starts=seq_starts, seq_ends=seq_ends),
        grid=(n_tok // tq,),
        in_specs=[
            pl.BlockSpec(memory_space=pltpu.SMEM),
            pl.BlockSpec((B_Q_HEADS, HEAD_DIM, tq), lambda i: (0, 0, i)),
            k_prev, k_cur, k_next, vt_prev, vt_cur, vt_next,
            pl.BlockSpec((B_KV_HEADS, BAND, GROUP * Q_BLOCK), lambda i: (0, 0, 0)),
        ],
        out_specs=pl.BlockSpec((tq, B_WIDTH), lambda i: (i, 0)),
        out_shape=jax.ShapeDtypeStruct((n_tok, B_WIDTH), BF16),
        compiler_params=_params("arbitrary"),
        name="window_attention",
    )(sink, qbt, kb, kb, kb, vbt, vbt, vbt, bias)


def _outproj_kernel(*refs, a_starts, x_starts):
    a_refs, refs = refs[:len(a_starts)], refs[len(a_starts):]
    x_refs, refs = refs[:len(x_starts)], refs[len(x_starts):]
    b_ref, mod_ref, ga_ref, gb_ref, w_ref, gm_ref, o_ref, h_ref = refs
    out_a = _read_parts(a_refs, a_starts)
    mix_a = (_rms(out_a.astype(F32)) * ga_ref[...]).astype(BF16)
    mix_b = (_rms(b_ref[...].astype(F32)) * gb_ref[...]).astype(BF16)
    y = (jnp.dot(mix_a, w_ref[:A_WIDTH, :], preferred_element_type=F32)
         + jnp.dot(mix_b, w_ref[A_WIDTH:, :], preferred_element_type=F32))
    x = _read_parts(x_refs, x_starts) + mod_ref[2:3, :] * y
    o_ref[...] = x
    h_ref[...] = (_rms(x) * gm_ref[...] * (1.0 + mod_ref[4:5, :]) + mod_ref[3:4, :]).astype(BF16)


def _out_projection(out_a_parts, out_b, x_parts, mod_rows, out_norm_a, out_norm_b, w_out, norm_mlp):
    n_tok = out_b.shape[0]
    tm = TOKEN_TILE
    n_tiles = n_tok // tm
    seg_tiles = n_tiles // mod_rows.shape[0]
    row = lambda i: (i, 0)
    const = lambda i: (0, 0)
    return pl.pallas_call(
        functools.partial(_outproj_kernel, a_starts=_part_starts(out_a_parts, tm),
                          x_starts=_part_starts(x_parts, tm)),
        grid=(n_tiles,),
        in_specs=[
            *_part_specs(out_a_parts, tm, A_WIDTH),
            *_part_specs(x_parts, tm, D_MODEL),
            pl.BlockSpec((tm, B_WIDTH), row),
            pl.BlockSpec((None, 6, D_MODEL), lambda i: (i // seg_tiles, 0, 0)),
            pl.BlockSpec((1, A_WIDTH), const),
            pl.BlockSpec((1, B_WIDTH), const),
            pl.BlockSpec((MIX_WIDTH, D_MODEL), const),
            pl.BlockSpec((1, D_MODEL), const),
        ],
        out_specs=[pl.BlockSpec((tm, D_MODEL), row), pl.BlockSpec((tm, D_MODEL), row)],
        out_shape=[jax.ShapeDtypeStruct((n_tok, D_MODEL), F32),
                   jax.ShapeDtypeStruct((n_tok, D_MODEL), BF16)],
        compiler_params=_params("arbitrary"),
        name="out_projection",
    )(*out_a_parts, *x_parts, out_b, mod_rows, out_norm_a.reshape(1, A_WIDTH),
      out_norm_b.reshape(1, B_WIDTH), w_out, norm_mlp.reshape(1, D_MODEL))


def _mlp_kernel(x_ref, h_ref, mod_ref, wu_ref, wd_ref, gf_ref, o_ref, *, final_norm):
    k = pl.program_id(1)

    @pl.when(k == 0)
    def _():
        o_ref[...] = jnp.zeros(o_ref.shape, F32)

    u = jnp.maximum(jnp.dot(h_ref[...], wu_ref[...], preferred_element_type=F32), 0.0)
    o_ref[...] += jnp.dot((u * u).astype(BF16), wd_ref[...], preferred_element_type=F32)

    @pl.when(k == pl.num_programs(1) - 1)
    def _():
        y = x_ref[...] + mod_ref[5:6, :] * o_ref[...]
        if final_norm:
            y = _rms(y) * gf_ref[...]
        o_ref[...] = y


def _mlp(x, h, mod_rows, w_up, w_down, norm_final, final_norm, tok0=0, n_out=None):
    n_tok = x.shape[0]
    n_out = n_tok if n_out is None else n_out
    tm, tf = MLP_TOKEN_TILE, FF_TILE
    seg = n_tok // mod_rows.shape[0]
    tm = min(tm, seg)
    assert tok0 % tm == 0 and n_out % tm == 0
    t0, seg_tiles = tok0 // tm, seg // tm
    return pl.pallas_call(
        functools.partial(_mlp_kernel, final_norm=final_norm),
        grid=(n_out // tm, D_FF // tf),
        in_specs=[
            pl.BlockSpec((tm, D_MODEL), lambda i, k: (t0 + i, 0)),
            pl.BlockSpec((tm, D_MODEL), lambda i, k: (t0 + i, 0)),
            pl.BlockSpec((None, 6, D_MODEL), lambda i, k: ((t0 + i) // seg_tiles, 0, 0)),
            pl.BlockSpec((D_MODEL, tf), lambda i, k: (0, k)),
            pl.BlockSpec((tf, D_MODEL), lambda i, k: (k, 0)),
            pl.BlockSpec((1, D_MODEL), lambda i, k: (0, 0)),
        ],
        out_specs=pl.BlockSpec((tm, D_MODEL), lambda i, k: (i, 0)),
        out_shape=jax.ShapeDtypeStruct((n_out, D_MODEL), F32),
        compiler_params=_params("arbitrary", "arbitrary"),
        name="mlp",
    )(x, h, mod_rows, w_up, w_down, norm_final.reshape(1, D_MODEL))


def kernel(x_prompt, x_sample, c_prompt, c_sample, w_mod, b_mod, norm_attn, w_in, q_norm, k_norm,
           sink, out_norm_a, out_norm_b, w_out, norm_mlp, w_up, w_down, rel_bias, norm_final):
    n_prompt, t_prompt, _ = x_prompt.shape
    n_sample, t_sample, _ = x_sample.shape
    tok_prompt = n_prompt * t_prompt
    n_tok = tok_prompt + n_sample * t_sample
    seg = math.gcd(t_prompt, t_sample)
    seg_row = ([b for b in range(n_prompt) for _ in range(t_prompt // seg)]
               + [n_prompt + b for b in range(n_sample) for _ in range(t_sample // seg)])
    seq_bounds = ([(b * t_prompt, (b + 1) * t_prompt) for b in range(n_prompt)]
                  + [(tok_prompt + b * t_sample, tok_prompt + (b + 1) * t_sample)
                     for b in range(n_sample)])

    x_parts = (x_prompt.reshape(tok_prompt, D_MODEL), x_sample.reshape(n_tok - tok_prompt, D_MODEL))
    c_rows = jnp.concatenate([c_prompt, c_sample], axis=0)
    c_rows = jnp.pad(c_rows, ((0, MOD_ROWS - c_rows.shape[0]), (0, 0)))
    mod = _modulation(c_rows, w_mod, b_mod)
    mod = mod.reshape(DEPTH, MOD_ROWS, 6, D_MODEL)[:, jnp.asarray(seg_row)]

    rope = _rope_tables(max(t_prompt, t_sample))
    tiles_prompt, tiles_sample = t_prompt // TOKEN_TILE, t_sample // TOKEN_TILE
    n_tiles_prompt = tok_prompt // TOKEN_TILE

    def rope_block(i):
        return jnp.where(i < n_tiles_prompt, i % tiles_prompt, (i - n_tiles_prompt) % tiles_sample)

    bias = _band_bias(rel_bias)
    w_in_b, w_out_b = w_in.astype(BF16), w_out.astype(BF16)
    w_up_b, w_down_b = w_up.astype(BF16), w_down.astype(BF16)

    for l in range(DEPTH):
        qat, ka, vat, qbt, kb, vbt = _in_projection(x_parts, mod[l], norm_attn[l], w_in_b[l],
                                                    q_norm[l], k_norm[l], rope, rope_block)
        out_a_parts = (_global_attention(qat, ka, vat, 0, t_prompt, n_prompt),
                       _global_attention(qat, ka, vat, tok_prompt, t_sample, n_sample))
        out_b = _window_attention(qbt, kb, vbt, bias, sink[l], seq_bounds)
        x, h = _out_projection(out_a_parts, out_b, x_parts, mod[l], out_norm_a[l], out_norm_b[l],
                               w_out_b[l], norm_mlp[l])
        mlp = functools.partial(_mlp, x, h, mod[l], w_up_b[l], w_down_b[l], norm_final)
        if l < DEPTH - 1:
            x_parts = (mlp(False),)
    y_prompt = mlp(True, 0, tok_prompt).reshape(n_prompt, t_prompt, D_MODEL)
    y_sample = mlp(True, tok_prompt, n_tok - tok_prompt).reshape(n_sample, t_sample, D_MODEL)
    return y_prompt, y_sample
```

```python
import functools
import math

import jax
import jax.numpy as jnp
from jax import lax
from jax.experimental import pallas as pl
from jax.experimental.pallas import tpu as pltpu

D_MODEL = 2048
DEPTH = 4
HEAD_DIM = 128
A_Q_HEADS = 8
A_KV_HEADS = 2
B_Q_HEADS = 8
B_KV_HEADS = 2
GROUP = A_Q_HEADS // A_KV_HEADS
A_WIDTH = A_Q_HEADS * HEAD_DIM
B_WIDTH = B_Q_HEADS * HEAD_DIM
MIX_WIDTH = A_WIDTH + B_WIDTH
KV_WIDTH = A_KV_HEADS * HEAD_DIM
IN_WIDTH = A_WIDTH + 2 * KV_WIDTH + B_WIDTH + 2 * KV_WIDTH
D_FF = 4 * D_MODEL
Q_BLOCK = 128
WINDOW = 128
BAND = Q_BLOCK + 2 * WINDOW
NUM_BUCKETS = 32
MAX_DISTANCE = 128
GRID_W = 64
ROPE_THETA = 10000.0
EPS = 1e-6
SCALE = HEAD_DIM ** -0.5
LOG2E = math.log2(math.e)

OFF_QA = 0
OFF_KA = OFF_QA + A_WIDTH
OFF_VA = OFF_KA + KV_WIDTH
OFF_QB = OFF_VA + KV_WIDTH
OFF_KB = OFF_QB + B_WIDTH
OFF_VB = OFF_KB + KV_WIDTH

MOD_ROWS = 8
TOKEN_TILE = 512
MLP_TOKEN_TILE = 1024
FF_TILE = 512
ATTN_Q_TILE = 256
ATTN_CHUNKS = 1
ATTN_SLOTS = 3
ATTN_ROUND_UNROLL = 5
SAFE_EXP2_RANGE = 60.0
SUBLANES = 8
WIN_Q_TILE = 512
VMEM_LIMIT = 60 * 1024 * 1024

F32 = jnp.float32
BF16 = jnp.bfloat16


def _params(*semantics):
    return pltpu.CompilerParams(dimension_semantics=semantics, vmem_limit_bytes=VMEM_LIMIT)


def _rms(x):
    return x * lax.rsqrt(jnp.mean(x * x, axis=-1, keepdims=True) + EPS)


def _part_starts(parts, tm):
    starts, start = [], 0
    for p in parts:
        starts.append(start)
        start += p.shape[0] // tm
    return tuple(starts)


def _part_specs(parts, tm, width):
    specs = []
    for p, start in zip(parts, _part_starts(parts, tm)):
        n = p.shape[0] // tm
        specs.append(pl.BlockSpec((tm, width),
                                  lambda i, start=start, n=n: (jnp.clip(i - start, 0, n - 1), 0)))
    return specs


def _read_parts(refs, starts):
    i = pl.program_id(0)
    x = refs[0][...]
    for ref, start in zip(refs[1:], starts[1:]):
        x = jnp.where(i >= start, ref[...], x)
    return x


def _mod_kernel(c_ref, w_ref, b_ref, o_ref):
    c = c_ref[...]
    a = (c * jax.nn.sigmoid(c)).astype(BF16)
    o_ref[...] = jnp.dot(a, w_ref[...].astype(BF16), preferred_element_type=F32) + b_ref[...]


def _modulation(c_rows, w_mod, b_mod):
    tn = D_MODEL
    return pl.pallas_call(
        _mod_kernel,
        grid=(DEPTH, 6 * D_MODEL // tn),
        in_specs=[
            pl.BlockSpec((MOD_ROWS, D_MODEL), lambda l, n: (0, 0)),
            pl.BlockSpec((None, D_MODEL, tn), lambda l, n: (l, 0, n)),
            pl.BlockSpec((None, 1, tn), lambda l, n: (l, 0, n)),
        ],
        out_specs=pl.BlockSpec((None, MOD_ROWS, tn), lambda l, n: (l, 0, n)),
        out_shape=jax.ShapeDtypeStruct((DEPTH, MOD_ROWS, 6 * D_MODEL), F32),
        compiler_params=_params("arbitrary", "arbitrary"),
        name="modulation",
    )(c_rows, w_mod, b_mod.reshape(DEPTH, 1, 6 * D_MODEL))


def _bias_kernel(bucket_ref, rel_bias_ref, o_ref):
    bucket = bucket_ref[...]
    in_band = bucket >= 0
    for h in range(B_Q_HEADS):
        acc = jnp.zeros((BAND, Q_BLOCK), F32)
        for b in range(NUM_BUCKETS):
            acc = jnp.where(bucket == b, rel_bias_ref[b, h], acc)
        g = h % GROUP
        o_ref[h // GROUP, :, g * Q_BLOCK:(g + 1) * Q_BLOCK] = jnp.where(in_band, acc * LOG2E, -jnp.inf)


def _t5_bucket(rel):
    half = NUM_BUCKETS // 2
    max_exact = half // 2
    n = jnp.abs(rel)
    nf = jnp.maximum(n, 1).astype(F32)
    large = max_exact + (jnp.log(nf / max_exact) / math.log(MAX_DISTANCE / max_exact)
                         * (half - max_exact)).astype(jnp.int32)
    large = jnp.minimum(large, half - 1)
    return jnp.where(rel > 0, half, 0) + jnp.where(n < max_exact, n, large)


def _band_bias(rel_bias):
    rel = jnp.arange(BAND)[:, None] - WINDOW - jnp.arange(Q_BLOCK)[None, :]
    bucket = jnp.where(jnp.abs(rel) <= WINDOW, _t5_bucket(rel), -1).astype(jnp.int32)
    return pl.pallas_call(
        _bias_kernel,
        in_specs=[
            pl.BlockSpec(memory_space=pltpu.VMEM),
            pl.BlockSpec(memory_space=pltpu.SMEM),
        ],
        out_specs=pl.BlockSpec(memory_space=pltpu.VMEM),
        out_shape=jax.ShapeDtypeStruct((B_KV_HEADS, BAND, GROUP * Q_BLOCK), F32),
        name="band_bias",
    )(bucket, rel_bias)


def _rope_tables(t_len):
    rows = t_len // GRID_W
    row_ids = jnp.repeat(jnp.arange(rows, dtype=F32), GRID_W)
    col_ids = jnp.tile(jnp.arange(GRID_W, dtype=F32), rows)
    half = HEAD_DIM // 2
    inv_freq = ROPE_THETA ** (-jnp.arange(0, half, 2, dtype=F32) / half)
    ang_r = row_ids[:, None] * inv_freq[None, :]
    ang_c = col_ids[:, None] * inv_freq[None, :]
    ang = jnp.concatenate([ang_r, ang_r, ang_c, ang_c], axis=-1)
    cos, sin = jnp.cos(ang), jnp.sin(ang)
    quarter = (jnp.arange(HEAD_DIM) // (HEAD_DIM // 4))[None, :]
    sin_up = jnp.where(quarter % 2 == 0, -sin, 0.0)
    sin_down = jnp.where(quarter % 2 == 1, sin, 0.0)
    return cos, sin_up, sin_down


def _inproj_kernel(*refs, x_starts):
    x_refs, refs = refs[:len(x_starts)], refs[len(x_starts):]
    (mod_ref, g_ref, w_ref, qn_ref, kn_ref, cos_ref, su_ref, sd_ref,
     qat_ref, ka_ref, vat_ref, qbt_ref, kb_ref, vbt_ref) = refs
    x = _read_parts(x_refs, x_starts)
    h = (_rms(x) * g_ref[...] * (1.0 + mod_ref[1:2, :]) + mod_ref[0:1, :]).astype(BF16)

    def proj(lo, width):
        return jnp.dot(h, w_ref[:, lo:lo + width], preferred_element_type=F32)

    cos, s_up, s_down = cos_ref[...], su_ref[...], sd_ref[...]
    quarter = HEAD_DIM // 4

    def norm_rope(z, gain):
        zn = _rms(z) * gain
        return (zn * cos + pltpu.roll(zn, HEAD_DIM - quarter, 1) * s_up
                + pltpu.roll(zn, quarter, 1) * s_down)

    qa = proj(OFF_QA, A_WIDTH)
    for hd in range(A_Q_HEADS):
        sl = slice(hd * HEAD_DIM, (hd + 1) * HEAD_DIM)
        qat_ref[hd] = (norm_rope(qa[:, sl], qn_ref[...]) * (SCALE * LOG2E)).T.astype(BF16)
    ka = proj(OFF_KA, KV_WIDTH)
    for hd in range(A_KV_HEADS):
        sl = slice(hd * HEAD_DIM, (hd + 1) * HEAD_DIM)
        ka_ref[:, sl] = norm_rope(ka[:, sl], kn_ref[...]).astype(BF16)
    va = proj(OFF_VA, KV_WIDTH)
    for hd in range(A_KV_HEADS):
        sl = slice(hd * HEAD_DIM, (hd + 1) * HEAD_DIM)
        vat_ref[hd, 0] = va[:, sl].T.astype(BF16)
    qb = proj(OFF_QB, B_WIDTH)
    for hd in range(B_Q_HEADS):
        sl = slice(hd * HEAD_DIM, (hd + 1) * HEAD_DIM)
        qbt_ref[hd] = (qb[:, sl] * (SCALE * LOG2E)).T.astype(BF16)
    kb_ref[...] = proj(OFF_KB, KV_WIDTH).astype(BF16)
    vb = proj(OFF_VB, KV_WIDTH)
    for hd in range(B_KV_HEADS):
        sl = slice(hd * HEAD_DIM, (hd + 1) * HEAD_DIM)
        vbt_ref[hd] = vb[:, sl].T.astype(BF16)


def _in_projection(x_parts, mod_rows, norm_attn, w_in, q_norm, k_norm, rope, rope_block):
    n_tok = sum(p.shape[0] for p in x_parts)
    tm = TOKEN_TILE
    n_tiles = n_tok // tm
    per_seg = mod_rows.shape[0]
    seg_tiles = n_tiles // per_seg
    cos, s_up, s_down = rope
    row = lambda i: (i, 0)
    const = lambda i: (0, 0)
    rope_spec = pl.BlockSpec((tm, HEAD_DIM), lambda i: (rope_block(i), 0))
    return pl.pallas_call(
        functools.partial(_inproj_kernel, x_starts=_part_starts(x_parts, tm)),
        grid=(n_tiles,),
        in_specs=[
            *_part_specs(x_parts, tm, D_MODEL),
            pl.BlockSpec((None, 6, D_MODEL), lambda i: (i // seg_tiles, 0, 0)),
            pl.BlockSpec((1, D_MODEL), const),
            pl.BlockSpec((D_MODEL, IN_WIDTH), const),
            pl.BlockSpec((1, HEAD_DIM), const),
            pl.BlockSpec((1, HEAD_DIM), const),
            rope_spec, rope_spec, rope_spec,
        ],
        out_specs=[
            pl.BlockSpec((A_Q_HEADS, HEAD_DIM, tm), lambda i: (0, 0, i)),
            pl.BlockSpec((tm, KV_WIDTH), row),
            pl.BlockSpec((A_KV_HEADS, 1, HEAD_DIM, tm), lambda i: (0, i, 0, 0)),
            pl.BlockSpec((B_Q_HEADS, HEAD_DIM, tm), lambda i: (0, 0, i)),
            pl.BlockSpec((tm, KV_WIDTH), row),
            pl.BlockSpec((B_KV_HEADS, HEAD_DIM, tm), lambda i: (0, 0, i)),
        ],
        out_shape=[
            jax.ShapeDtypeStruct((A_Q_HEADS, HEAD_DIM, n_tok), BF16),
            jax.ShapeDtypeStruct((n_tok, KV_WIDTH), BF16),
            jax.ShapeDtypeStruct((A_KV_HEADS, n_tiles, HEAD_DIM, tm), BF16),
            jax.ShapeDtypeStruct((B_Q_HEADS, HEAD_DIM, n_tok), BF16),
            jax.ShapeDtypeStruct((n_tok, KV_WIDTH), BF16),
            jax.ShapeDtypeStruct((B_KV_HEADS, HEAD_DIM, n_tok), BF16),
        ],
        compiler_params=_params("arbitrary"),
        name="in_projection",
    )(*x_parts, mod_rows, norm_attn.reshape(1, D_MODEL), w_in, q_norm.reshape(1, HEAD_DIM),
      k_norm.reshape(1, HEAD_DIM), cos, s_up, s_down)


def _global_attn_kernel(q_ref, k_ref, vt_ref, o_ref, qt_ref, *scratch, n_key_tiles):
    tq = q_ref.shape[-1]
    tk = vt_ref.shape[-1]
    n_lanes = GROUP * tq
    rows = tk // ATTN_CHUNKS
    cols = n_lanes // ATTN_CHUNKS
    n = ATTN_SLOTS
    s_bufs, p_bufs, a_bufs, c_bufs = (scratch[i * n:(i + 1) * n] for i in range(4))
    m_ref, l_ref, acc_ref, ksq_ref = scratch[4 * n:]
    q_sq = []
    for g in range(GROUP):
        qt_ref[:, g * tq:(g + 1) * tq] = q_ref[g]
        qg = q_ref[g].astype(F32)
        q_sq.append(jnp.max(jnp.sum(qg * qg, axis=0, keepdims=True)))

    @pl.when(pl.program_id(2) == 0)
    def _():
        def tile_max(j, best):
            kj = k_ref[pl.ds(pl.multiple_of(j * tk, tk), tk), :].astype(F32)
            return jnp.maximum(best, jnp.max(jnp.sum(kj * kj, axis=-1, keepdims=True)))

        ksq_ref[0] = lax.fori_loop(0, n_key_tiles, tile_max, jnp.float32(0.0))

    bounded = functools.reduce(jnp.maximum, q_sq) * ksq_ref[0] <= SAFE_EXP2_RANGE ** 2
    l_ref[...] = jnp.zeros(l_ref.shape, F32)
    acc_ref[...] = jnp.zeros(acc_ref.shape, F32)

    def fold(x, op):
        return functools.reduce(op, [x[r:r + SUBLANES] for r in range(0, x.shape[0], SUBLANES)])

    def rounds(first, last, one_step, per_round=n):
        n_rounds = (last - first) // per_round

        def round_(i, carry):
            for u in range(per_round):
                one_step(first + per_round * i + u, first + u)
            return carry

        if n_rounds > 1:
            lax.fori_loop(0, n_rounds, round_, 0)
        elif n_rounds == 1:
            round_(0, 0)
        for t in range(first + n_rounds * per_round, last):
            one_step(t, t)

    def bounded_step(qk, pv):
        if qk is not None:
            l_part = l_ref[...]
        for c in range(ATTN_CHUNKS):
            rs = slice(c * rows, (c + 1) * rows)
            cs = slice(c * cols, (c + 1) * cols)
            if qk is not None:
                j, slot = qk
                kc = k_ref[pl.ds(pl.multiple_of(j * tk + c * rows, rows), rows), :]
                p = jnp.exp2(jnp.dot(kc, qt_ref[...], preferred_element_type=F32))
                l_part = l_part + fold(p, jnp.add)
                p_bufs[slot][rs, :] = p.astype(BF16)
            if pv is not None:
                j, slot = pv
                acc_ref[:, cs] += jnp.dot(vt_ref[j], p_bufs[slot][:, cs],
                                          preferred_element_type=F32)
        if qk is not None:
            l_ref[...] = l_part

    @pl.when(bounded)
    def _():
        bounded_step((0, 0), None)
        rounds(0, n_key_tiles - 1,
               lambda t, ts: bounded_step((t + 1, (ts + 1) % n), (t, ts % n)),
               per_round=n * ATTN_ROUND_UNROLL)
        bounded_step(None, (n_key_tiles - 1, (n_key_tiles - 1) % n))

    def step(qk, sm, pv):
        if sm is not None:
            m_prev = m_ref[...]
            m_new = jnp.maximum(m_prev, c_bufs[sm][...])
            alpha = jnp.exp2(m_prev - m_new)
            a_bufs[sm][...] = alpha
            m_ref[...] = m_new
            l_part = alpha * l_ref[...]
        if qk is not None:
            col_max = jnp.full((SUBLANES, n_lanes), -jnp.inf, F32)
        for c in range(ATTN_CHUNKS):
            rs = slice(c * rows, (c + 1) * rows)
            cs = slice(c * cols, (c + 1) * cols)
            if qk is not None:
                j, slot = qk
                kc = k_ref[pl.ds(pl.multiple_of(j * tk + c * rows, rows), rows), :]
                sc = jnp.dot(kc, qt_ref[...], preferred_element_type=F32)
                s_bufs[slot][rs, :] = sc
                col_max = jnp.maximum(col_max, fold(sc, jnp.maximum))
            if sm is not None:
                p = jnp.exp2(s_bufs[sm][rs, :] - m_new)
                l_part = l_part + fold(p, jnp.add)
                p_bufs[sm][rs, :] = p.astype(BF16)
            if pv is not None:
                j, slot = pv
                acc_ref[:, cs] = a_bufs[slot][:, cs] * acc_ref[:, cs] + jnp.dot(
                    vt_ref[j], p_bufs[slot][:, cs], preferred_element_type=F32)
        if qk is not None:
            c_bufs[qk[1]][...] = jnp.max(col_max, axis=0, keepdims=True)
        if sm is not None:
            l_ref[...] = l_part

    @pl.when(jnp.logical_not(bounded))
    def _():
        m_ref[...] = jnp.full(m_ref.shape, -jnp.inf, F32)
        step((0, 0), None, None)
        step((1, 1 % n), 0, None)
        rounds(1, n_key_tiles - 1,
               lambda t, ts: step((t + 1, (ts + 1) % n), ts % n, (t - 1, (ts - 1) % n)))
        step(None, (n_key_tiles - 1) % n, (n_key_tiles - 2, (n_key_tiles - 2) % n))
        step(None, None, (n_key_tiles - 1, (n_key_tiles - 1) % n))

    o = acc_ref[...] / jnp.sum(l_ref[...], axis=0, keepdims=True)
    for g in range(GROUP):
        o_ref[:, g * HEAD_DIM:(g + 1) * HEAD_DIM] = o[:, g * tq:(g + 1) * tq].T.astype(BF16)


def _global_attention(qat, ka, vat, tok0, seq_len, n_seq):
    tq, tk = ATTN_Q_TILE, TOKEN_TILE
    q_tiles = seq_len // tq
    k_tiles = seq_len // tk
    assert tok0 % seq_len == 0 and k_tiles >= 2
    q0 = tok0 // tq
    s0 = tok0 // seq_len
    n_lanes = GROUP * tq
    return pl.pallas_call(
        functools.partial(_global_attn_kernel, n_key_tiles=k_tiles),
        grid=(n_seq, A_KV_HEADS, q_tiles),
        in_specs=[
            pl.BlockSpec((GROUP, HEAD_DIM, tq), lambda b, h, i: (h, 0, q0 + b * q_tiles + i)),
            pl.BlockSpec((seq_len, HEAD_DIM), lambda b, h, i: (s0 + b, h)),
            pl.BlockSpec((None, k_tiles, HEAD_DIM, tk), lambda b, h, i: (h, s0 + b, 0, 0)),
        ],
        out_specs=pl.BlockSpec((tq, GROUP * HEAD_DIM), lambda b, h, i: (b * q_tiles + i, h)),
        out_shape=jax.ShapeDtypeStruct((n_seq * seq_len, A_WIDTH), BF16),
        scratch_shapes=[
            pltpu.VMEM((HEAD_DIM, n_lanes), BF16),
            *[pltpu.VMEM((tk, n_lanes), F32)] * ATTN_SLOTS,
            *[pltpu.VMEM((tk, n_lanes), BF16)] * ATTN_SLOTS,
            *[pltpu.VMEM((1, n_lanes), F32)] * ATTN_SLOTS,
            *[pltpu.VMEM((1, n_lanes), F32)] * ATTN_SLOTS,
            pltpu.VMEM((1, n_lanes), F32),
            pltpu.VMEM((SUBLANES, n_lanes), F32),
            pltpu.VMEM((HEAD_DIM, n_lanes), F32),
            pltpu.SMEM((1,), F32),
        ],
        compiler_params=_params("arbitrary", "arbitrary", "arbitrary"),
        name="global_attention",
    )(qat, ka, vat)


def _window_attn_kernel(sink_ref, qt_ref, kp_ref, kc_ref, kn_ref, vtp_ref, vtc_ref, vtn_ref, bias_ref,
                        o_ref, *, seq_starts, seq_ends):
    i = pl.program_id(0)
    sub_blocks = qt_ref.shape[-1] // Q_BLOCK
    n_lanes = GROUP * Q_BLOCK
    lane_group = lax.broadcasted_iota(jnp.int32, (1, n_lanes), 1) // Q_BLOCK
    neg_inf = jnp.full((WINDOW, n_lanes), -jnp.inf, F32)

    def fold(x, op):
        return functools.reduce(op, [x[r:r + SUBLANES] for r in range(0, x.shape[0], SUBLANES)])

    for h in range(B_KV_HEADS):
        hs = slice(h * HEAD_DIM, (h + 1) * HEAD_DIM)
        k_all = jnp.concatenate([kp_ref[:, hs], kc_ref[:, hs], kn_ref[:, hs]], axis=0)
        vt_all = jnp.concatenate([vtp_ref[h], vtc_ref[h], vtn_ref[h]], axis=1)
        sink = jnp.full((1, n_lanes), sink_ref[h * GROUP] * LOG2E, F32)
        for g in range(1, GROUP):
            sink = jnp.where(lane_group == g, sink_ref[h * GROUP + g] * LOG2E, sink)
        for u in range(sub_blocks):
            blk = i * sub_blocks + u
            first = functools.reduce(jnp.logical_or, [blk == s for s in seq_starts])
            last = functools.reduce(jnp.logical_or, [blk == e - 1 for e in seq_ends])
            qs = slice(u * Q_BLOCK, (u + 1) * Q_BLOCK)
            qt = jnp.concatenate([qt_ref[h * GROUP + g, :, qs] for g in range(GROUP)], axis=1)
            kw = k_all[u * Q_BLOCK:u * Q_BLOCK + BAND]
            s = jnp.dot(kw, qt, preferred_element_type=F32) + bias_ref[h]
            s_prev = jnp.where(first, neg_inf, s[:WINDOW])
            s_mid = s[WINDOW:WINDOW + Q_BLOCK]
            s_next = jnp.where(last, neg_inf, s[WINDOW + Q_BLOCK:])
            col_max = functools.reduce(
                jnp.maximum, [fold(x, jnp.maximum) for x in (s_prev, s_mid, s_next)])
            m = jnp.maximum(jnp.max(col_max, axis=0, keepdims=True), sink)
            e = [jnp.exp2(x - m) for x in (s_prev, s_mid, s_next)]
            col_sum = functools.reduce(jnp.add, [fold(x, jnp.add) for x in e])
            den = jnp.sum(col_sum, axis=0, keepdims=True) + jnp.exp2(sink - m)
            p = jnp.concatenate([x.astype(BF16) for x in e], axis=0)
            ot = jnp.dot(vt_all[:, u * Q_BLOCK:u * Q_BLOCK + BAND], p,
                         preferred_element_type=F32) / den
            for g in range(GROUP):
                o_ref[qs, (h * GROUP + g) * HEAD_DIM:(h * GROUP + g + 1) * HEAD_DIM] = (
                    ot[:, g * Q_BLOCK:(g + 1) * Q_BLOCK].T.astype(BF16))


def _window_attention(qbt, kb, vbt, bias, sink, seq_bounds):
    n_tok = kb.shape[0]
    tq = WIN_Q_TILE
    sub = tq // Q_BLOCK
    n_blocks = n_tok // Q_BLOCK
    seq_starts = tuple(s // Q_BLOCK for s, _ in seq_bounds)
    seq_ends = tuple(e // Q_BLOCK for _, e in seq_bounds)
    prev_block = lambda i: jnp.maximum(i * sub - 1, 0)
    next_block = lambda i: jnp.minimum(i * sub + sub, n_blocks - 1)
    k_cur = pl.BlockSpec((tq, KV_WIDTH), lambda i: (i, 0))
    k_prev = pl.BlockSpec((Q_BLOCK, KV_WIDTH), lambda i: (prev_block(i), 0))
    k_next = pl.BlockSpec((Q_BLOCK, KV_WIDTH), lambda i: (next_block(i), 0))
    vt_cur = pl.BlockSpec((B_KV_HEADS, HEAD_DIM, tq), lambda i: (0, 0, i))
    vt_prev = pl.BlockSpec((B_KV_HEADS, HEAD_DIM, Q_BLOCK), lambda i: (0, 0, prev_block(i)))
    vt_next = pl.BlockSpec((B_KV_HEADS, HEAD_DIM, Q_BLOCK), lambda i: (0, 0, next_block(i)))
    return pl.pallas_call(
        functools.partial(_window_attn_kernel, seq_starts=seq_starts, seq_ends=seq_ends),
        grid=(n_tok // tq,),
        in_specs=[
            pl.BlockSpec(memory_space=pltpu.SMEM),
            pl.BlockSpec((B_Q_HEADS, HEAD_DIM, tq), lambda i: (0, 0, i)),
            k_prev, k_cur, k_next, vt_prev, vt_cur, vt_next,
            pl.BlockSpec((B_KV_HEADS, BAND, GROUP * Q_BLOCK), lambda i: (0, 0, 0)),
        ],
        out_specs=pl.BlockSpec((tq, B_WIDTH), lambda i: (i, 0)),
        out_shape=jax.ShapeDtypeStruct((n_tok, B_WIDTH), BF16),
        compiler_params=_params("arbitrary"),
        name="window_attention",
    )(sink, qbt, kb, kb, kb, vbt, vbt, vbt, bias)


def _outproj_kernel(*refs, a_starts, x_starts):
    a_refs, refs = refs[:len(a_starts)], refs[len(a_starts):]
    x_refs, refs = refs[:len(x_starts)], refs[len(x_starts):]
    b_ref, mod_ref, ga_ref, gb_ref, w_ref, gm_ref, o_ref, h_ref = refs
    out_a = _read_parts(a_refs, a_starts)
    mix_a = (_rms(out_a.astype(F32)) * ga_ref[...]).astype(BF16)
    mix_b = (_rms(b_ref[...].astype(F32)) * gb_ref[...]).astype(BF16)
    y = (jnp.dot(mix_a, w_ref[:A_WIDTH, :], preferred_element_type=F32)
         + jnp.dot(mix_b, w_ref[A_WIDTH:, :], preferred_element_type=F32))
    x = _read_parts(x_refs, x_starts) + mod_ref[2:3, :] * y
    o_ref[...] = x
    h_ref[...] = (_rms(x) * gm_ref[...] * (1.0 + mod_ref[4:5, :]) + mod_ref[3:4, :]).astype(BF16)


def _out_projection(out_a_parts, out_b, x_parts, mod_rows, out_norm_a, out_norm_b, w_out, norm_mlp):
    n_tok = out_b.shape[0]
    tm = TOKEN_TILE
    n_tiles = n_tok // tm
    seg_tiles = n_tiles // mod_rows.shape[0]
    row = lambda i: (i, 0)
    const = lambda i: (0, 0)
    return pl.pallas_call(
        functools.partial(_outproj_kernel, a_starts=_part_starts(out_a_parts, tm),
                          x_starts=_part_starts(x_parts, tm)),
        grid=(n_tiles,),
        in_specs=[
            *_part_specs(out_a_parts, tm, A_WIDTH),
            *_part_specs(x_parts, tm, D_MODEL),
            pl.BlockSpec((tm, B_WIDTH), row),
            pl.BlockSpec((None, 6, D_MODEL), lambda i: (i // seg_tiles, 0, 0)),
            pl.BlockSpec((1, A_WIDTH), const),
            pl.BlockSpec((1, B_WIDTH), const),
            pl.BlockSpec((MIX_WIDTH, D_MODEL), const),
            pl.BlockSpec((1, D_MODEL), const),
        ],
        out_specs=[pl.BlockSpec((tm, D_MODEL), row), pl.BlockSpec((tm, D_MODEL), row)],
        out_shape=[jax.ShapeDtypeStruct((n_tok, D_MODEL), F32),
                   jax.ShapeDtypeStruct((n_tok, D_MODEL), BF16)],
        compiler_params=_params("arbitrary"),
        name="out_projection",
    )(*out_a_parts, *x_parts, out_b, mod_rows, out_norm_a.reshape(1, A_WIDTH),
      out_norm_b.reshape(1, B_WIDTH), w_out, norm_mlp.reshape(1, D_MODEL))


def _mlp_kernel(x_ref, h_ref, mod_ref, wu_ref, wd_ref, gf_ref, o_ref, *, final_norm):
    k = pl.program_id(1)

    @pl.when(k == 0)
    def _():
        o_ref[...] = jnp.zeros(o_ref.shape, F32)

    u = jnp.maximum(jnp.dot(h_ref[...], wu_ref[...], preferred_element_type=F32), 0.0)
    o_ref[...] += jnp.dot((u * u).astype(BF16), wd_ref[...], preferred_element_type=F32)

    @pl.when(k == pl.num_programs(1) - 1)
    def _():
        y = x_ref[...] + mod_ref[5:6, :] * o_ref[...]
        if final_norm:
            y = _rms(y) * gf_ref[...]
        o_ref[...] = y


def _mlp(x, h, mod_rows, w_up, w_down, norm_final, final_norm, tok0=0, n_out=None):
    n_tok = x.shape[0]
    n_out = n_tok if n_out is None else n_out
    tm, tf = MLP_TOKEN_TILE, FF_TILE
    seg = n_tok // mod_rows.shape[0]
    tm = min(tm, seg)
    assert tok0 % tm == 0 and n_out % tm == 0
    t0, seg_tiles = tok0 // tm, seg // tm
    return pl.pallas_call(
        functools.partial(_mlp_kernel, final_norm=final_norm),
        grid=(n_out // tm, D_FF // tf),
        in_specs=[
            pl.BlockSpec((tm, D_MODEL), lambda i, k: (t0 + i, 0)),
            pl.BlockSpec((tm, D_MODEL), lambda i, k: (t0 + i, 0)),
            pl.BlockSpec((None, 6, D_MODEL), lambda i, k: ((t0 + i) // seg_tiles, 0, 0)),
            pl.BlockSpec((D_MODEL, tf), lambda i, k: (0, k)),
            pl.BlockSpec((tf, D_MODEL), lambda i, k: (k, 0)),
            pl.BlockSpec((1, D_MODEL), lambda i, k: (0, 0)),
        ],
        out_specs=pl.BlockSpec((tm, D_MODEL), lambda i, k: (i, 0)),
        out_shape=jax.ShapeDtypeStruct((n_out, D_MODEL), F32),
        compiler_params=_params("arbitrary", "arbitrary"),
        name="mlp",
    )(x, h, mod_rows, w_up, w_down, norm_final.reshape(1, D_MODEL))


def kernel(x_prompt, x_sample, c_prompt, c_sample, w_mod, b_mod, norm_attn, w_in, q_norm, k_norm,
           sink, out_norm_a, out_norm_b, w_out, norm_mlp, w_up, w_down, rel_bias, norm_final):
    n_prompt, t_prompt, _ = x_prompt.shape
    n_sample, t_sample, _ = x_sample.shape
    tok_prompt = n_prompt * t_prompt
    n_tok = tok_prompt + n_sample * t_sample
    seg = math.gcd(t_prompt, t_sample)
    seg_row = ([b for b in range(n_prompt) for _ in range(t_prompt // seg)]
               + [n_prompt + b for b in range(n_sample) for _ in range(t_sample // seg)])
    seq_bounds = ([(b * t_prompt, (b + 1) * t_prompt) for b in range(n_prompt)]
                  + [(tok_prompt + b * t_sample, tok_prompt + (b + 1) * t_sample)
                     for b in range(n_sample)])

    x_parts = (x_prompt.reshape(tok_prompt, D_MODEL), x_sample.reshape(n_tok - tok_prompt, D_MODEL))
    c_rows = jnp.concatenate([c_prompt, c_sample], axis=0)
    c_rows = jnp.pad(c_rows, ((0, MOD_ROWS - c_rows.shape[0]), (0, 0)))
    mod = _modulation(c_rows, w_mod, b_mod)
    mod = mod.reshape(DEPTH, MOD_ROWS, 6, D_MODEL)[:, jnp.asarray(seg_row)]

    rope = _rope_tables(max(t_prompt, t_sample))
    tiles_prompt, tiles_sample = t_prompt // TOKEN_TILE, t_sample // TOKEN_TILE
    n_tiles_prompt = tok_prompt // TOKEN_TILE

    def rope_block(i):
        return jnp.where(i < n_tiles_prompt, i % tiles_prompt, (i - n_tiles_prompt) % tiles_sample)

    bias = _band_bias(rel_bias)
    w_in_b, w_out_b = w_in.astype(BF16), w_out.astype(BF16)
    w_up_b, w_down_b = w_up.astype(BF16), w_down.astype(BF16)

    for l in range(DEPTH):
        qat, ka, vat, qbt, kb, vbt = _in_projection(x_parts, mod[l], norm_attn[l], w_in_b[l],
                                                    q_norm[l], k_norm[l], rope, rope_block)
        out_a_parts = (_global_attention(qat, ka, vat, 0, t_prompt, n_prompt),
                       _global_attention(qat, ka, vat, tok_prompt, t_sample, n_sample))
        out_b = _window_attention(qbt, kb, vbt, bias, sink[l], seq_bounds)
        x, h = _out_projection(out_a_parts, out_b, x_parts, mod[l], out_norm_a[l], out_norm_b[l],
                               w_out_b[l], norm_mlp[l])
        mlp = functools.partial(_mlp, x, h, mod[l], w_up_b[l], w_down_b[l], norm_final)
        if l < DEPTH - 1:
            x_parts = (mlp(False),)
    y_prompt = mlp(True, 0, tok_prompt).reshape(n_prompt, t_prompt, D_MODEL)
    y_sample = mlp(True, tok_prompt, n_tok - tok_prompt).reshape(n_sample, t_sample, D_MODEL)
    return y_prompt, y_sample
```

```python
import functools
import math

import jax
import jax.numpy as jnp
from jax import lax
from jax.experimental import pallas as pl
from jax.experimental.pallas import tpu as pltpu

D_MODEL = 2048
DEPTH = 4
HEAD_DIM = 128
A_Q_HEADS = 8
A_KV_HEADS = 2
B_Q_HEADS = 8
B_KV_HEADS = 2
GROUP = A_Q_HEADS // A_KV_HEADS
A_WIDTH = A_Q_HEADS * HEAD_DIM
B_WIDTH = B_Q_HEADS * HEAD_DIM
MIX_WIDTH = A_WIDTH + B_WIDTH
KV_WIDTH = A_KV_HEADS * HEAD_DIM
IN_WIDTH = A_WIDTH + 2 * KV_WIDTH + B_WIDTH + 2 * KV_WIDTH
D_FF = 4 * D_MODEL
Q_BLOCK = 128
WINDOW = 128
BAND = Q_BLOCK + 2 * WINDOW
NUM_BUCKETS = 32
MAX_DISTANCE = 128
GRID_W = 64
ROPE_THETA = 10000.0
EPS = 1e-6
SCALE = HEAD_DIM ** -0.5
LOG2E = math.log2(math.e)

OFF_QA = 0
OFF_KA = OFF_QA + A_WIDTH
OFF_VA = OFF_KA + KV_WIDTH
OFF_QB = OFF_VA + KV_WIDTH
OFF_KB = OFF_QB + B_WIDTH
OFF_VB = OFF_KB + KV_WIDTH

MOD_ROWS = 8
TOKEN_TILE = 512
MLP_TOKEN_TILE = 1024
FF_TILE = 512
ATTN_Q_TILE = 256
ATTN_CHUNKS = 1
ATTN_SLOTS = 3
ATTN_ROUND_UNROLL = 11
SAFE_EXP2_RANGE = 60.0
SUBLANES = 8
WIN_Q_TILE = 1024
VMEM_LIMIT = 60 * 1024 * 1024

F32 = jnp.float32
BF16 = jnp.bfloat16


def _params(*semantics):
    return pltpu.CompilerParams(dimension_semantics=semantics, vmem_limit_bytes=VMEM_LIMIT)


def _rms(x):
    return x * lax.rsqrt(jnp.mean(x * x, axis=-1, keepdims=True) + EPS)


def _part_starts(parts, tm):
    starts, start = [], 0
    for p in parts:
        starts.append(start)
        start += p.shape[0] // tm
    return tuple(starts)


def _part_specs(parts, tm, width):
    specs = []
    for p, start in zip(parts, _part_starts(parts, tm)):
        n = p.shape[0] // tm
        specs.append(pl.BlockSpec((tm, width),
                                  lambda i, start=start, n=n: (jnp.clip(i - start, 0, n - 1), 0)))
    return specs


def _read_parts(refs, starts):
    i = pl.program_id(0)
    x = refs[0][...]
    for ref, start in zip(refs[1:], starts[1:]):
        x = jnp.where(i >= start, ref[...], x)
    return x


def _mod_kernel(c_ref, w_ref, b_ref, o_ref):
    c = c_ref[...]
    a = (c * jax.nn.sigmoid(c)).astype(BF16)
    o_ref[...] = jnp.dot(a, w_ref[...].astype(BF16), preferred_element_type=F32) + b_ref[...]


def _modulation(c_rows, w_mod, b_mod):
    tn = D_MODEL
    return pl.pallas_call(
        _mod_kernel,
        grid=(DEPTH, 6 * D_MODEL // tn),
        in_specs=[
            pl.BlockSpec((MOD_ROWS, D_MODEL), lambda l, n: (0, 0)),
            pl.BlockSpec((None, D_MODEL, tn), lambda l, n: (l, 0, n)),
            pl.BlockSpec((None, 1, tn), lambda l, n: (l, 0, n)),
        ],
        out_specs=pl.BlockSpec((None, MOD_ROWS, tn), lambda l, n: (l, 0, n)),
        out_shape=jax.ShapeDtypeStruct((DEPTH, MOD_ROWS, 6 * D_MODEL), F32),
        compiler_params=_params("arbitrary", "arbitrary"),
        name="modulation",
    )(c_rows, w_mod, b_mod.reshape(DEPTH, 1, 6 * D_MODEL))


def _bias_kernel(bucket_ref, rel_bias_ref, o_ref):
    bucket = bucket_ref[...]
    in_band = bucket >= 0
    for h in range(B_Q_HEADS):
        acc = jnp.zeros((BAND, Q_BLOCK), F32)
        for b in range(NUM_BUCKETS):
            acc = jnp.where(bucket == b, rel_bias_ref[b, h], acc)
        g = h % GROUP
        o_ref[h // GROUP, :, g * Q_BLOCK:(g + 1) * Q_BLOCK] = jnp.where(in_band, acc * LOG2E, -jnp.inf)


def _t5_bucket(rel):
    half = NUM_BUCKETS // 2
    max_exact = half // 2
    n = jnp.abs(rel)
    nf = jnp.maximum(n, 1).astype(F32)
    large = max_exact + (jnp.log(nf / max_exact) / math.log(MAX_DISTANCE / max_exact)
                         * (half - max_exact)).astype(jnp.int32)
    large = jnp.minimum(large, half - 1)
    return jnp.where(rel > 0, half, 0) + jnp.where(n < max_exact, n, large)


def _band_bias(rel_bias):
    rel = jnp.arange(BAND)[:, None] - WINDOW - jnp.arange(Q_BLOCK)[None, :]
    bucket = jnp.where(jnp.abs(rel) <= WINDOW, _t5_bucket(rel), -1).astype(jnp.int32)
    return pl.pallas_call(
        _bias_kernel,
        in_specs=[
            pl.BlockSpec(memory_space=pltpu.VMEM),
            pl.BlockSpec(memory_space=pltpu.SMEM),
        ],
        out_specs=pl.BlockSpec(memory_space=pltpu.VMEM),
        out_shape=jax.ShapeDtypeStruct((B_KV_HEADS, BAND, GROUP * Q_BLOCK), F32),
        name="band_bias",
    )(bucket, rel_bias)


def _rope_tables(t_len):
    rows = t_len // GRID_W
    row_ids = jnp.repeat(jnp.arange(rows, dtype=F32), GRID_W)
    col_ids = jnp.tile(jnp.arange(GRID_W, dtype=F32), rows)
    half = HEAD_DIM // 2
    inv_freq = ROPE_THETA ** (-jnp.arange(0, half, 2, dtype=F32) / half)
    ang_r = row_ids[:, None] * inv_freq[None, :]
    ang_c = col_ids[:, None] * inv_freq[None, :]
    ang = jnp.concatenate([ang_r, ang_r, ang_c, ang_c], axis=-1)
    cos, sin = jnp.cos(ang), jnp.sin(ang)
    quarter = (jnp.arange(HEAD_DIM) // (HEAD_DIM // 4))[None, :]
    sin_up = jnp.where(quarter % 2 == 0, -sin, 0.0)
    sin_down = jnp.where(quarter % 2 == 1, sin, 0.0)
    return cos, sin_up, sin_down


def _inproj_kernel(*refs, x_starts):
    x_refs, refs = refs[:len(x_starts)], refs[len(x_starts):]
    (mod_ref, g_ref, w_ref, qn_ref, kn_ref, cos_ref, su_ref, sd_ref,
     qat_ref, ka_ref, vat_ref, qbt_ref, kb_ref, vbt_ref) = refs
    x = _read_parts(x_refs, x_starts)
    h = (_rms(x) * g_ref[...] * (1.0 + mod_ref[1:2, :]) + mod_ref[0:1, :]).astype(BF16)

    def proj(lo, width):
        return jnp.dot(h, w_ref[:, lo:lo + width], preferred_element_type=F32)

    cos, s_up, s_down = cos_ref[...], su_ref[...], sd_ref[...]
    quarter = HEAD_DIM // 4

    def norm_rope(z, gain):
        zn = _rms(z) * gain
        return (zn * cos + pltpu.roll(zn, HEAD_DIM - quarter, 1) * s_up
                + pltpu.roll(zn, quarter, 1) * s_down)

    qa = proj(OFF_QA, A_WIDTH)
    for hd in range(A_Q_HEADS):
        sl = slice(hd * HEAD_DIM, (hd + 1) * HEAD_DIM)
        qat_ref[hd] = (norm_rope(qa[:, sl], qn_ref[...]) * (SCALE * LOG2E)).T.astype(BF16)
    ka = proj(OFF_KA, KV_WIDTH)
    for hd in range(A_KV_HEADS):
        sl = slice(hd * HEAD_DIM, (hd + 1) * HEAD_DIM)
        ka_ref[:, sl] = norm_rope(ka[:, sl], kn_ref[...]).astype(BF16)
    va = proj(OFF_VA, KV_WIDTH)
    for hd in range(A_KV_HEADS):
        sl = slice(hd * HEAD_DIM, (hd + 1) * HEAD_DIM)
        vat_ref[hd, 0] = va[:, sl].T.astype(BF16)
    qb = proj(OFF_QB, B_WIDTH)
    for hd in range(B_Q_HEADS):
        sl = slice(hd * HEAD_DIM, (hd + 1) * HEAD_DIM)
        qbt_ref[hd] = (qb[:, sl] * (SCALE * LOG2E)).T.astype(BF16)
    kb_ref[...] = proj(OFF_KB, KV_WIDTH).astype(BF16)
    vb = proj(OFF_VB, KV_WIDTH)
    for hd in range(B_KV_HEADS):
        sl = slice(hd * HEAD_DIM, (hd + 1) * HEAD_DIM)
        vbt_ref[hd] = vb[:, sl].T.astype(BF16)


def _in_projection(x_parts, mod_rows, norm_attn, w_in, q_norm, k_norm, rope, rope_block):
    n_tok = sum(p.shape[0] for p in x_parts)
    tm = TOKEN_TILE
    n_tiles = n_tok // tm
    per_seg = mod_rows.shape[0]
    seg_tiles = n_tiles // per_seg
    cos, s_up, s_down = rope
    row = lambda i: (i, 0)
    const = lambda i: (0, 0)
    rope_spec = pl.BlockSpec((tm, HEAD_DIM), lambda i: (rope_block(i), 0))
    return pl.pallas_call(
        functools.partial(_inproj_kernel, x_starts=_part_starts(x_parts, tm)),
        grid=(n_tiles,),
        in_specs=[
            *_part_specs(x_parts, tm, D_MODEL),
            pl.BlockSpec((None, 6, D_MODEL), lambda i: (i // seg_tiles, 0, 0)),
            pl.BlockSpec((1, D_MODEL), const),
            pl.BlockSpec((D_MODEL, IN_WIDTH), const),
            pl.BlockSpec((1, HEAD_DIM), const),
            pl.BlockSpec((1, HEAD_DIM), const),
            rope_spec, rope_spec, rope_spec,
        ],
        out_specs=[
            pl.BlockSpec((A_Q_HEADS, HEAD_DIM, tm), lambda i: (0, 0, i)),
            pl.BlockSpec((tm, KV_WIDTH), row),
            pl.BlockSpec((A_KV_HEADS, 1, HEAD_DIM, tm), lambda i: (0, i, 0, 0)),
            pl.BlockSpec((B_Q_HEADS, HEAD_DIM, tm), lambda i: (0, 0, i)),
            pl.BlockSpec((tm, KV_WIDTH), row),
            pl.BlockSpec((B_KV_HEADS, HEAD_DIM, tm), lambda i: (0, 0, i)),
        ],
        out_shape=[
            jax.ShapeDtypeStruct((A_Q_HEADS, HEAD_DIM, n_tok), BF16),
            jax.ShapeDtypeStruct((n_tok, KV_WIDTH), BF16),
            jax.ShapeDtypeStruct((A_KV_HEADS, n_tiles, HEAD_DIM, tm), BF16),
            jax.ShapeDtypeStruct((B_Q_HEADS, HEAD_DIM, n_tok), BF16),
            jax.ShapeDtypeStruct((n_tok, KV_WIDTH), BF16),
            jax.ShapeDtypeStruct((B_KV_HEADS, HEAD_DIM, n_tok), BF16),
        ],
        compiler_params=_params("arbitrary"),
        name="in_projection",
    )(*x_parts, mod_rows, norm_attn.reshape(1, D_MODEL), w_in, q_norm.reshape(1, HEAD_DIM),
      k_norm.reshape(1, HEAD_DIM), cos, s_up, s_down)


def _global_attn_kernel(bound_ref, q_ref, k_ref, vt_ref, o_ref, qt_ref, *scratch, n_key_tiles):
    tq = q_ref.shape[-1]
    tk = vt_ref.shape[-1]
    n_lanes = GROUP * tq
    rows = tk // ATTN_CHUNKS
    cols = n_lanes // ATTN_CHUNKS
    n = ATTN_SLOTS
    s_bufs, p_bufs, a_bufs, c_bufs = (scratch[i * n:(i + 1) * n] for i in range(4))
    m_ref, l_ref, acc_ref = scratch[4 * n:]
    for g in range(GROUP):
        qt_ref[:, g * tq:(g + 1) * tq] = q_ref[g]
    bounded = bound_ref[0] <= SAFE_EXP2_RANGE
    l_ref[...] = jnp.zeros(l_ref.shape, F32)
    acc_ref[...] = jnp.zeros(acc_ref.shape, F32)

    def fold(x, op):
        return functools.reduce(op, [x[r:r + SUBLANES] for r in range(0, x.shape[0], SUBLANES)])

    def rounds(first, last, one_step, per_round=n):
        n_rounds = (last - first) // per_round

        def round_(i, carry):
            for u in range(per_round):
                one_step(first + per_round * i + u, first + u)
            return carry

        if n_rounds > 1:
            lax.fori_loop(0, n_rounds, round_, 0)
        elif n_rounds == 1:
            round_(0, 0)
        for t in range(first + n_rounds * per_round, last):
            one_step(t, t)

    def bounded_step(qk, pv):
        if qk is not None:
            l_part = l_ref[...]
        for c in range(ATTN_CHUNKS):
            rs = slice(c * rows, (c + 1) * rows)
            cs = slice(c * cols, (c + 1) * cols)
            if qk is not None:
                j, slot = qk
                kc = k_ref[pl.ds(pl.multiple_of(j * tk + c * rows, rows), rows), :]
                p = jnp.exp2(jnp.dot(kc, qt_ref[...], preferred_element_type=F32))
                l_part = l_part + fold(p, jnp.add)
                p_bufs[slot][rs, :] = p.astype(BF16)
            if pv is not None:
                j, slot = pv
                acc_ref[:, cs] += jnp.dot(vt_ref[j], p_bufs[slot][:, cs],
                                          preferred_element_type=F32)
        if qk is not None:
            l_ref[...] = l_part

    @pl.when(bounded)
    def _():
        bounded_step((0, 0), None)
        rounds(0, n_key_tiles - 1,
               lambda t, ts: bounded_step((t + 1, (ts + 1) % n), (t, ts % n)),
               per_round=n * ATTN_ROUND_UNROLL)
        bounded_step(None, (n_key_tiles - 1, (n_key_tiles - 1) % n))

    def step(qk, sm, pv):
        if sm is not None:
            m_prev = m_ref[...]
            m_new = jnp.maximum(m_prev, c_bufs[sm][...])
            alpha = jnp.exp2(m_prev - m_new)
            a_bufs[sm][...] = alpha
            m_ref[...] = m_new
            l_part = alpha * l_ref[...]
        if qk is not None:
            col_max = jnp.full((SUBLANES, n_lanes), -jnp.inf, F32)
        for c in range(ATTN_CHUNKS):
            rs = slice(c * rows, (c + 1) * rows)
            cs = slice(c * cols, (c + 1) * cols)
            if qk is not None:
                j, slot = qk
                kc = k_ref[pl.ds(pl.multiple_of(j * tk + c * rows, rows), rows), :]
                sc = jnp.dot(kc, qt_ref[...], preferred_element_type=F32)
                s_bufs[slot][rs, :] = sc
                col_max = jnp.maximum(col_max, fold(sc, jnp.maximum))
            if sm is not None:
                p = jnp.exp2(s_bufs[sm][rs, :] - m_new)
                l_part = l_part + fold(p, jnp.add)
                p_bufs[sm][rs, :] = p.astype(BF16)
            if pv is not None:
                j, slot = pv
                acc_ref[:, cs] = a_bufs[slot][:, cs] * acc_ref[:, cs] + jnp.dot(
                    vt_ref[j], p_bufs[slot][:, cs], preferred_element_type=F32)
        if qk is not None:
            c_bufs[qk[1]][...] = jnp.max(col_max, axis=0, keepdims=True)
        if sm is not None:
            l_ref[...] = l_part

    @pl.when(jnp.logical_not(bounded))
    def _():
        m_ref[...] = jnp.full(m_ref.shape, -jnp.inf, F32)
        step((0, 0), None, None)
        step((1, 1 % n), 0, None)
        rounds(1, n_key_tiles - 1,
               lambda t, ts: step((t + 1, (ts + 1) % n), ts % n, (t - 1, (ts - 1) % n)))
        step(None, (n_key_tiles - 1) % n, (n_key_tiles - 2, (n_key_tiles - 2) % n))
        step(None, None, (n_key_tiles - 1, (n_key_tiles - 1) % n))

    o = acc_ref[...] / jnp.sum(l_ref[...], axis=0, keepdims=True)
    for g in range(GROUP):
        o_ref[:, g * HEAD_DIM:(g + 1) * HEAD_DIM] = o[:, g * tq:(g + 1) * tq].T.astype(BF16)


def _score_bound(q_norm, k_norm):
    return (HEAD_DIM * SCALE * LOG2E * jnp.max(jnp.abs(q_norm)) * jnp.max(jnp.abs(k_norm))
            ).astype(F32).reshape(1)


def _global_attention(score_bound, qat, ka, vat, tok0, seq_len, n_seq):
    tq, tk = ATTN_Q_TILE, TOKEN_TILE
    q_tiles = seq_len // tq
    k_tiles = seq_len // tk
    assert tok0 % seq_len == 0 and k_tiles >= 2
    q0 = tok0 // tq
    s0 = tok0 // seq_len
    n_lanes = GROUP * tq
    return pl.pallas_call(
        functools.partial(_global_attn_kernel, n_key_tiles=k_tiles),
        grid=(n_seq, A_KV_HEADS, q_tiles),
        in_specs=[
            pl.BlockSpec(memory_space=pltpu.SMEM),
            pl.BlockSpec((GROUP, HEAD_DIM, tq), lambda b, h, i: (h, 0, q0 + b * q_tiles + i)),
            pl.BlockSpec((seq_len, HEAD_DIM), lambda b, h, i: (s0 + b, h)),
            pl.BlockSpec((None, k_tiles, HEAD_DIM, tk), lambda b, h, i: (h, s0 + b, 0, 0)),
        ],
        out_specs=pl.BlockSpec((tq, GROUP * HEAD_DIM), lambda b, h, i: (b * q_tiles + i, h)),
        out_shape=jax.ShapeDtypeStruct((n_seq * seq_len, A_WIDTH), BF16),
        scratch_shapes=[
            pltpu.VMEM((HEAD_DIM, n_lanes), BF16),
            *[pltpu.VMEM((tk, n_lanes), F32)] * ATTN_SLOTS,
            *[pltpu.VMEM((tk, n_lanes), BF16)] * ATTN_SLOTS,
            *[pltpu.VMEM((1, n_lanes), F32)] * ATTN_SLOTS,
            *[pltpu.VMEM((1, n_lanes), F32)] * ATTN_SLOTS,
            pltpu.VMEM((1, n_lanes), F32),
            pltpu.VMEM((SUBLANES, n_lanes), F32),
            pltpu.VMEM((HEAD_DIM, n_lanes), F32),
        ],
        compiler_params=_params("arbitrary", "arbitrary", "arbitrary"),
        name="global_attention",
    )(score_bound, qat, ka, vat)


def _window_attn_kernel(sink_ref, qt_ref, kp_ref, kc_ref, kn_ref, vtp_ref, vtc_ref, vtn_ref, bias_ref,
                        o_ref, *, seq_starts, seq_ends):
    i = pl.program_id(0)
    sub_blocks = qt_ref.shape[-1] // Q_BLOCK
    n_lanes = GROUP * Q_BLOCK
    lane_group = lax.broadcasted_iota(jnp.int32, (1, n_lanes), 1) // Q_BLOCK
    neg_inf = jnp.full((WINDOW, n_lanes), -jnp.inf, F32)

    def fold(x, op):
        return functools.reduce(op, [x[r:r + SUBLANES] for r in range(0, x.shape[0], SUBLANES)])

    for h in range(B_KV_HEADS):
        hs = slice(h * HEAD_DIM, (h + 1) * HEAD_DIM)
        k_all = jnp.concatenate([kp_ref[:, hs], kc_ref[:, hs], kn_ref[:, hs]], axis=0)
        vt_all = jnp.concatenate([vtp_ref[h], vtc_ref[h], vtn_ref[h]], axis=1)
        sink = jnp.full((1, n_lanes), sink_ref[h * GROUP] * LOG2E, F32)
        for g in range(1, GROUP):
            sink = jnp.where(lane_group == g, sink_ref[h * GROUP + g] * LOG2E, sink)
        for u in range(sub_blocks):
            blk = i * sub_blocks + u
            first = functools.reduce(jnp.logical_or, [blk == s for s in seq_starts])
            last = functools.reduce(jnp.logical_or, [blk == e - 1 for e in seq_ends])
            qs = slice(u * Q_BLOCK, (u + 1) * Q_BLOCK)
            qt = jnp.concatenate([qt_ref[h * GROUP + g, :, qs] for g in range(GROUP)], axis=1)
            kw = k_all[u * Q_BLOCK:u * Q_BLOCK + BAND]
            s = jnp.dot(kw, qt, preferred_element_type=F32) + bias_ref[h]
            s_prev = jnp.where(first, neg_inf, s[:WINDOW])
            s_mid = s[WINDOW:WINDOW + Q_BLOCK]
            s_next = jnp.where(last, neg_inf, s[WINDOW + Q_BLOCK:])
            col_max = functools.reduce(
                jnp.maximum, [fold(x, jnp.maximum) for x in (s_prev, s_mid, s_next)])
            m = jnp.maximum(jnp.max(col_max, axis=0, keepdims=True), sink)
            e = [jnp.exp2(x - m) for x in (s_prev, s_mid, s_next)]
            col_sum = functools.reduce(jnp.add, [fold(x, jnp.add) for x in e])
            den = jnp.sum(col_sum, axis=0, keepdims=True) + jnp.exp2(sink - m)
            p = jnp.concatenate([x.astype(BF16) for x in e], axis=0)
            ot = jnp.dot(vt_all[:, u * Q_BLOCK:u * Q_BLOCK + BAND], p,
                         preferred_element_type=F32) / den
            for g in range(GROUP):
                o_ref[qs, (h * GROUP + g) * HEAD_DIM:(h * GROUP + g + 1) * HEAD_DIM] = (
                    ot[:, g * Q_BLOCK:(g + 1) * Q_BLOCK].T.astype(BF16))


def _window_attention(qbt, kb, vbt, bias, sink, seq_bounds):
    n_tok = kb.shape[0]
    tq = WIN_Q_TILE
    sub = tq // Q_BLOCK
    n_blocks = n_tok // Q_BLOCK
    seq_starts = tuple(s // Q_BLOCK for s, _ in seq_bounds)
    seq_ends = tuple(e // Q_BLOCK for _, e in seq_bounds)
    prev_block = lambda i: jnp.maximum(i * sub - 1, 0)
    next_block = lambda i: jnp.minimum(i * sub + sub, n_blocks - 1)
    k_cur = pl.BlockSpec((tq, KV_WIDTH), lambda i: (i, 0))
    k_prev = pl.BlockSpec((Q_BLOCK, KV_WIDTH), lambda i: (prev_block(i), 0))
    k_next = pl.BlockSpec((Q_BLOCK, KV_WIDTH), lambda i: (next_block(i), 0))
    vt_cur = pl.BlockSpec((B_KV_HEADS, HEAD_DIM, tq), lambda i: (0, 0, i))
    vt_prev = pl.BlockSpec((B_KV_HEADS, HEAD_DIM, Q_BLOCK), lambda i: (0, 0, prev_block(i)))
    vt_next = pl.BlockSpec((B_KV_HEADS, HEAD_DIM, Q_BLOCK), lambda i: (0, 0, next_block(i)))
    return pl.pallas_call(
        functools.partial(_window_attn_kernel, seq_starts=seq_starts, seq_ends=seq_ends),
        grid=(n_tok // tq,),
        in_specs=[
            pl.BlockSpec(memory_space=pltpu.SMEM),
            pl.BlockSpec((B_Q_HEADS, HEAD_DIM, tq), lambda i: (0, 0, i)),
            k_prev, k_cur, k_next, vt_prev, vt_cur, vt_next,
            pl.BlockSpec((B_KV_HEADS, BAND, GROUP * Q_BLOCK), lambda i: (0, 0, 0)),
        ],
        out_specs=pl.BlockSpec((tq, B_WIDTH), lambda i: (i, 0)),
        out_shape=jax.ShapeDtypeStruct((n_tok, B_WIDTH), BF16),
        compiler_params=_params("arbitrary"),
        name="window_attention",
    )(sink, qbt, kb, kb, kb, vbt, vbt, vbt, bias)


def _outproj_kernel(*refs, a_starts, x_starts):
    a_refs, refs = refs[:len(a_starts)], refs[len(a_starts):]
    x_refs, refs = refs[:len(x_starts)], refs[len(x_starts):]
    b_ref, mod_ref, ga_ref, gb_ref, w_ref, gm_ref, o_ref, h_ref = refs
    out_a = _read_parts(a_refs, a_starts)
    mix_a = (_rms(out_a.astype(F32)) * ga_ref[...]).astype(BF16)
    mix_b = (_rms(b_ref[...].astype(F32)) * gb_ref[...]).astype(BF16)
    y = (jnp.dot(mix_a, w_ref[:A_WIDTH, :], preferred_element_type=F32)
         + jnp.dot(mix_b, w_ref[A_WIDTH:, :], preferred_element_type=F32))
    x = _read_parts(x_refs, x_starts) + mod_ref[2:3, :] * y
    o_ref[...] = x
    h_ref[...] = (_rms(x) * gm_ref[...] * (1.0 + mod_ref[4:5, :]) + mod_ref[3:4, :]).astype(BF16)


def _out_projection(out_a_parts, out_b, x_parts, mod_rows, out_norm_a, out_norm_b, w_out, norm_mlp):
    n_tok = out_b.shape[0]
    tm = TOKEN_TILE
    n_tiles = n_tok // tm
    seg_tiles = n_tiles // mod_rows.shape[0]
    row = lambda i: (i, 0)
    const = lambda i: (0, 0)
    return pl.pallas_call(
        functools.partial(_outproj_kernel, a_starts=_part_starts(out_a_parts, tm),
                          x_starts=_part_starts(x_parts, tm)),
        grid=(n_tiles,),
        in_specs=[
            *_part_specs(out_a_parts, tm, A_WIDTH),
            *_part_specs(x_parts, tm, D_MODEL),
            pl.BlockSpec((tm, B_WIDTH), row),
            pl.BlockSpec((None, 6, D_MODEL), lambda i: (i // seg_tiles, 0, 0)),
            pl.BlockSpec((1, A_WIDTH), const),
            pl.BlockSpec((1, B_WIDTH), const),
            pl.BlockSpec((MIX_WIDTH, D_MODEL), const),
            pl.BlockSpec((1, D_MODEL), const),
        ],
        out_specs=[pl.BlockSpec((tm, D_MODEL), row), pl.BlockSpec((tm, D_MODEL), row)],
        out_shape=[jax.ShapeDtypeStruct((n_tok, D_MODEL), F32),
                   jax.ShapeDtypeStruct((n_tok, D_MODEL), BF16)],
        compiler_params=_params("arbitrary"),
        name="out_projection",
    )(*out_a_parts, *x_parts, out_b, mod_rows, out_norm_a.reshape(1, A_WIDTH),
      out_norm_b.reshape(1, B_WIDTH), w_out, norm_mlp.reshape(1, D_MODEL))


def _mlp_kernel(x_ref, h_ref, mod_ref, wu_ref, wd_ref, gf_ref, o_ref, *, final_norm):
    k = pl.program_id(1)

    @pl.when(k == 0)
    def _():
        o_ref[...] = jnp.zeros(o_ref.shape, F32)

    u = jnp.maximum(jnp.dot(h_ref[...], wu_ref[...], preferred_element_type=F32), 0.0)
    o_ref[...] += jnp.dot((u * u).astype(BF16), wd_ref[...], preferred_element_type=F32)

    @pl.when(k == pl.num_programs(1) - 1)
    def _():
        y = x_ref[...] + mod_ref[5:6, :] * o_ref[...]
        if final_norm:
            y = _rms(y) * gf_ref[...]
        o_ref[...] = y


def _mlp(x, h, mod_rows, w_up, w_down, norm_final, final_norm, tok0=0, n_out=None):
    n_tok = x.shape[0]
    n_out = n_tok if n_out is None else n_out
    tm, tf = MLP_TOKEN_TILE, FF_TILE
    seg = n_tok // mod_rows.shape[0]
    tm = min(tm, seg)
    assert tok0 % tm == 0 and n_out % tm == 0
    t0, seg_tiles = tok0 // tm, seg // tm
    return pl.pallas_call(
        functools.partial(_mlp_kernel, final_norm=final_norm),
        grid=(n_out // tm, D_FF // tf),
        in_specs=[
            pl.BlockSpec((tm, D_MODEL), lambda i, k: (t0 + i, 0)),
            pl.BlockSpec((tm, D_MODEL), lambda i, k: (t0 + i, 0)),
            pl.BlockSpec((None, 6, D_MODEL), lambda i, k: ((t0 + i) // seg_tiles, 0, 0)),
            pl.BlockSpec((D_MODEL, tf), lambda i, k: (0, k)),
            pl.BlockSpec((tf, D_MODEL), lambda i, k: (k, 0)),
            pl.BlockSpec((1, D_MODEL), lambda i, k: (0, 0)),
        ],
        out_specs=pl.BlockSpec((tm, D_MODEL), lambda i, k: (i, 0)),
        out_shape=jax.ShapeDtypeStruct((n_out, D_MODEL), F32),
        compiler_params=_params("arbitrary", "arbitrary"),
        name="mlp",
    )(x, h, mod_rows, w_up, w_down, norm_final.reshape(1, D_MODEL))


def kernel(x_prompt, x_sample, c_prompt, c_sample, w_mod, b_mod, norm_attn, w_in, q_norm, k_norm,
           sink, out_norm_a, out_norm_b, w_out, norm_mlp, w_up, w_down, rel_bias, norm_final):
    n_prompt, t_prompt, _ = x_prompt.shape
    n_sample, t_sample, _ = x_sample.shape
    tok_prompt = n_prompt * t_prompt
    n_tok = tok_prompt + n_sample * t_sample
    seg = math.gcd(t_prompt, t_sample)
    seg_row = ([b for b in range(n_prompt) for _ in range(t_prompt // seg)]
               + [n_prompt + b for b in range(n_sample) for _ in range(t_sample // seg)])
    seq_bounds = ([(b * t_prompt, (b + 1) * t_prompt) for b in range(n_prompt)]
                  + [(tok_prompt + b * t_sample, tok_prompt + (b + 1) * t_sample)
                     for b in range(n_sample)])

    x_parts = (x_prompt.reshape(tok_prompt, D_MODEL), x_sample.reshape(n_tok - tok_prompt, D_MODEL))
    c_rows = jnp.concatenate([c_prompt, c_sample], axis=0)
    c_rows = jnp.pad(c_rows, ((0, MOD_ROWS - c_rows.shape[0]), (0, 0)))
    mod = _modulation(c_rows, w_mod, b_mod)
    mod = mod.reshape(DEPTH, MOD_ROWS, 6, D_MODEL)[:, jnp.asarray(seg_row)]

    rope = _rope_tables(max(t_prompt, t_sample))
    tiles_prompt, tiles_sample = t_prompt // TOKEN_TILE, t_sample // TOKEN_TILE
    n_tiles_prompt = tok_prompt // TOKEN_TILE

    def rope_block(i):
        return jnp.where(i < n_tiles_prompt, i % tiles_prompt, (i - n_tiles_prompt) % tiles_sample)

    bias = _band_bias(rel_bias)
    w_in_b, w_out_b = w_in.astype(BF16), w_out.astype(BF16)
    w_up_b, w_down_b = w_up.astype(BF16), w_down.astype(BF16)

    for l in range(DEPTH):
        qat, ka, vat, qbt, kb, vbt = _in_projection(x_parts, mod[l], norm_attn[l], w_in_b[l],
                                                    q_norm[l], k_norm[l], rope, rope_block)
        bound = _score_bound(q_norm[l], k_norm[l])
        out_a_parts = (_global_attention(bound, qat, ka, vat, 0, t_prompt, n_prompt),
                       _global_attention(bound, qat, ka, vat, tok_prompt, t_sample, n_sample))
        out_b = _window_attention(qbt, kb, vbt, bias, sink[l], seq_bounds)
        x, h = _out_projection(out_a_parts, out_b, x_parts, mod[l], out_norm_a[l], out_norm_b[l],
                               w_out_b[l], norm_mlp[l])
        mlp = functools.partial(_mlp, x, h, mod[l], w_up_b[l], w_down_b[l], norm_final)
        if l < DEPTH - 1:
            x_parts = (mlp(False),)
    y_prompt = mlp(True, 0, tok_prompt).reshape(n_prompt, t_prompt, D_MODEL)
    y_sample = mlp(True, tok_prompt, n_tok - tok_prompt).reshape(n_sample, t_sample, D_MODEL)
    return y_prompt, y_sample
```

```python
import functools
import math

import jax
import jax.numpy as jnp
from jax import lax
from jax.experimental import pallas as pl
from jax.experimental.pallas import tpu as pltpu

D_MODEL = 2048
DEPTH = 4
HEAD_DIM = 128
A_Q_HEADS = 8
A_KV_HEADS = 2
B_Q_HEADS = 8
B_KV_HEADS = 2
GROUP = A_Q_HEADS // A_KV_HEADS
A_WIDTH = A_Q_HEADS * HEAD_DIM
B_WIDTH = B_Q_HEADS * HEAD_DIM
MIX_WIDTH = A_WIDTH + B_WIDTH
KV_WIDTH = A_KV_HEADS * HEAD_DIM
IN_WIDTH = A_WIDTH + 2 * KV_WIDTH + B_WIDTH + 2 * KV_WIDTH
D_FF = 4 * D_MODEL
Q_BLOCK = 128
WINDOW = 128
BAND = Q_BLOCK + 2 * WINDOW
NUM_BUCKETS = 32
MAX_DISTANCE = 128
GRID_W = 64
ROPE_THETA = 10000.0
EPS = 1e-6
SCALE = HEAD_DIM ** -0.5
LOG2E = math.log2(math.e)

OFF_QA = 0
OFF_KA = OFF_QA + A_WIDTH
OFF_VA = OFF_KA + KV_WIDTH
OFF_QB = OFF_VA + KV_WIDTH
OFF_KB = OFF_QB + B_WIDTH
OFF_VB = OFF_KB + KV_WIDTH

MOD_ROWS = 8
N_MOD = 6
MOD_SHIFT_MIX, MOD_SCALE_MIX, MOD_GATE_MIX, MOD_SHIFT_MLP, MOD_SCALE_MLP, MOD_GATE_MLP = range(N_MOD)
TOKEN_TILE = 512
MLP_TOKEN_TILE = 1024
FF_TILE = 512
ATTN_Q_TILE = 256
ATTN_CHUNKS = 1
ATTN_SLOTS = 3
ATTN_ROUND_UNROLL = 11
SAFE_EXP2_RANGE = 60.0
SUBLANES = 8
VMEM_LIMIT = 60 * 1024 * 1024

F32 = jnp.float32
BF16 = jnp.bfloat16


def _params(*semantics):
    return pltpu.CompilerParams(dimension_semantics=semantics, vmem_limit_bytes=VMEM_LIMIT)


def _rms(x):
    return x * lax.rsqrt(jnp.mean(x * x, axis=-1, keepdims=True) + EPS)


def _part_starts(parts, tm):
    starts, start = [], 0
    for p in parts:
        starts.append(start)
        start += p.shape[0] // tm
    return tuple(starts)


def _part_specs(parts, tm, width):
    specs = []
    for p, start in zip(parts, _part_starts(parts, tm)):
        n = p.shape[0] // tm
        specs.append(pl.BlockSpec((tm, width),
                                  lambda i, start=start, n=n: (jnp.clip(i - start, 0, n - 1), 0)))
    return specs


def _read_parts(refs, starts):
    i = pl.program_id(0)
    x = refs[0][...]
    for ref, start in zip(refs[1:], starts[1:]):
        x = jnp.where(i >= start, ref[...], x)
    return x


def _mod_kernel(c_ref, w_ref, b_ref, o_ref):
    c = c_ref[...]
    a = (c * jax.nn.sigmoid(c)).astype(BF16)
    o_ref[...] = jnp.dot(a, w_ref[...].astype(BF16), preferred_element_type=F32) + b_ref[...]


def _modulation(c_rows, w_mod, b_mod):
    tn = D_MODEL
    return pl.pallas_call(
        _mod_kernel,
        grid=(DEPTH, N_MOD * D_MODEL // tn),
        in_specs=[
            pl.BlockSpec((MOD_ROWS, D_MODEL), lambda l, n: (0, 0)),
            pl.BlockSpec((None, D_MODEL, tn), lambda l, n: (l, 0, n)),
            pl.BlockSpec((None, 1, tn), lambda l, n: (l, 0, n)),
        ],
        out_specs=pl.BlockSpec((None, MOD_ROWS, tn), lambda l, n: (l, 0, n)),
        out_shape=jax.ShapeDtypeStruct((DEPTH, MOD_ROWS, N_MOD * D_MODEL), F32),
        compiler_params=_params("arbitrary", "arbitrary"),
        name="modulation",
    )(c_rows, w_mod, b_mod.reshape(DEPTH, 1, N_MOD * D_MODEL))


def _bias_kernel(bucket_ref, rel_bias_ref, o_ref):
    bucket = bucket_ref[...]
    in_band = bucket >= 0
    for h in range(B_Q_HEADS):
        acc = jnp.zeros((BAND, Q_BLOCK), F32)
        for b in range(NUM_BUCKETS):
            acc = jnp.where(bucket == b, rel_bias_ref[b, h], acc)
        g = h % GROUP
        o_ref[h // GROUP, :, g * Q_BLOCK:(g + 1) * Q_BLOCK] = jnp.where(in_band, acc * LOG2E, -jnp.inf)


def _t5_bucket(rel):
    half = NUM_BUCKETS // 2
    max_exact = half // 2
    n = jnp.abs(rel)
    nf = jnp.maximum(n, 1).astype(F32)
    large = max_exact + (jnp.log(nf / max_exact) / math.log(MAX_DISTANCE / max_exact)
                         * (half - max_exact)).astype(jnp.int32)
    large = jnp.minimum(large, half - 1)
    return jnp.where(rel > 0, half, 0) + jnp.where(n < max_exact, n, large)


def _band_bias(rel_bias):
    rel = jnp.arange(BAND)[:, None] - WINDOW - jnp.arange(Q_BLOCK)[None, :]
    bucket = jnp.where(jnp.abs(rel) <= WINDOW, _t5_bucket(rel), -1).astype(jnp.int32)
    return pl.pallas_call(
        _bias_kernel,
        in_specs=[
            pl.BlockSpec(memory_space=pltpu.VMEM),
            pl.BlockSpec(memory_space=pltpu.SMEM),
        ],
        out_specs=pl.BlockSpec(memory_space=pltpu.VMEM),
        out_shape=jax.ShapeDtypeStruct((B_KV_HEADS, BAND, GROUP * Q_BLOCK), F32),
        name="band_bias",
    )(bucket, rel_bias)


def _rope_tables(t_len):
    rows = t_len // GRID_W
    row_ids = jnp.repeat(jnp.arange(rows, dtype=F32), GRID_W)
    col_ids = jnp.tile(jnp.arange(GRID_W, dtype=F32), rows)
    half = HEAD_DIM // 2
    inv_freq = ROPE_THETA ** (-jnp.arange(0, half, 2, dtype=F32) / half)
    ang_r = row_ids[:, None] * inv_freq[None, :]
    ang_c = col_ids[:, None] * inv_freq[None, :]
    ang = jnp.concatenate([ang_r, ang_r, ang_c, ang_c], axis=-1)
    cos, sin = jnp.cos(ang), jnp.sin(ang)
    quarter = (jnp.arange(HEAD_DIM) // (HEAD_DIM // 4))[None, :]
    sin_up = jnp.where(quarter % 2 == 0, -sin, 0.0)
    sin_down = jnp.where(quarter % 2 == 1, sin, 0.0)
    return cos, sin_up, sin_down


def _inproj_kernel(*refs, x_starts):
    x_refs, refs = refs[:len(x_starts)], refs[len(x_starts):]
    (mod_ref, g_ref, w_ref, qn_ref, kn_ref, cos_ref, su_ref, sd_ref,
     qat_ref, ka_ref, vat_ref, qbt_ref, kb_ref, vbt_ref) = refs
    x = _read_parts(x_refs, x_starts)
    h = (_rms(x) * g_ref[...] * (1.0 + mod_ref[MOD_SCALE_MIX:MOD_SCALE_MIX + 1, :])
         + mod_ref[MOD_SHIFT_MIX:MOD_SHIFT_MIX + 1, :]).astype(BF16)

    def proj(lo, width):
        return jnp.dot(h, w_ref[:, lo:lo + width], preferred_element_type=F32)

    cos, s_up, s_down = cos_ref[...], su_ref[...], sd_ref[...]
    quarter = HEAD_DIM // 4

    def norm_rope(z, gain):
        zn = _rms(z) * gain
        return (zn * cos + pltpu.roll(zn, HEAD_DIM - quarter, 1) * s_up
                + pltpu.roll(zn, quarter, 1) * s_down)

    qa = proj(OFF_QA, A_WIDTH)
    for hd in range(A_Q_HEADS):
        sl = slice(hd * HEAD_DIM, (hd + 1) * HEAD_DIM)
        qat_ref[hd] = (norm_rope(qa[:, sl], qn_ref[...]) * (SCALE * LOG2E)).T.astype(BF16)
    ka = proj(OFF_KA, KV_WIDTH)
    for hd in range(A_KV_HEADS):
        sl = slice(hd * HEAD_DIM, (hd + 1) * HEAD_DIM)
        ka_ref[:, sl] = norm_rope(ka[:, sl], kn_ref[...]).astype(BF16)
    va = proj(OFF_VA, KV_WIDTH)
    for hd in range(A_KV_HEADS):
        sl = slice(hd * HEAD_DIM, (hd + 1) * HEAD_DIM)
        vat_ref[hd, 0] = va[:, sl].T.astype(BF16)
    qb = proj(OFF_QB, B_WIDTH)
    for hd in range(B_Q_HEADS):
        sl = slice(hd * HEAD_DIM, (hd + 1) * HEAD_DIM)
        qbt_ref[hd] = (qb[:, sl] * (SCALE * LOG2E)).T.astype(BF16)
    kb_ref[...] = proj(OFF_KB, KV_WIDTH).astype(BF16)
    vb = proj(OFF_VB, KV_WIDTH)
    for hd in range(B_KV_HEADS):
        sl = slice(hd * HEAD_DIM, (hd + 1) * HEAD_DIM)
        vbt_ref[hd] = vb[:, sl].T.astype(BF16)


def _in_projection(x_parts, mod_rows, norm_attn, w_in, q_norm, k_norm, rope, rope_block):
    n_tok = sum(p.shape[0] for p in x_parts)
    tm = TOKEN_TILE
    n_tiles = n_tok // tm
    per_seg = mod_rows.shape[0]
    seg_tiles = n_tiles // per_seg
    cos, s_up, s_down = rope
    row = lambda i: (i, 0)
    const = lambda i: (0, 0)
    rope_spec = pl.BlockSpec((tm, HEAD_DIM), lambda i: (rope_block(i), 0))
    return pl.pallas_call(
        functools.partial(_inproj_kernel, x_starts=_part_starts(x_parts, tm)),
        grid=(n_tiles,),
        in_specs=[
            *_part_specs(x_parts, tm, D_MODEL),
            pl.BlockSpec((None, N_MOD, D_MODEL), lambda i: (i // seg_tiles, 0, 0)),
            pl.BlockSpec((1, D_MODEL), const),
            pl.BlockSpec((D_MODEL, IN_WIDTH), const),
            pl.BlockSpec((1, HEAD_DIM), const),
            pl.BlockSpec((1, HEAD_DIM), const),
            rope_spec, rope_spec, rope_spec,
        ],
        out_specs=[
            pl.BlockSpec((A_Q_HEADS, HEAD_DIM, tm), lambda i: (0, 0, i)),
            pl.BlockSpec((tm, KV_WIDTH), row),
            pl.BlockSpec((A_KV_HEADS, 1, HEAD_DIM, tm), lambda i: (0, i, 0, 0)),
            pl.BlockSpec((B_Q_HEADS, HEAD_DIM, tm), lambda i: (0, 0, i)),
            pl.BlockSpec((tm, KV_WIDTH), row),
            pl.BlockSpec((B_KV_HEADS, HEAD_DIM, tm), lambda i: (0, 0, i)),
        ],
        out_shape=[
            jax.ShapeDtypeStruct((A_Q_HEADS, HEAD_DIM, n_tok), BF16),
            jax.ShapeDtypeStruct((n_tok, KV_WIDTH), BF16),
            jax.ShapeDtypeStruct((A_KV_HEADS, n_tiles, HEAD_DIM, tm), BF16),
            jax.ShapeDtypeStruct((B_Q_HEADS, HEAD_DIM, n_tok), BF16),
            jax.ShapeDtypeStruct((n_tok, KV_WIDTH), BF16),
            jax.ShapeDtypeStruct((B_KV_HEADS, HEAD_DIM, n_tok), BF16),
        ],
        compiler_params=_params("arbitrary"),
        name="in_projection",
    )(*x_parts, mod_rows, norm_attn.reshape(1, D_MODEL), w_in, q_norm.reshape(1, HEAD_DIM),
      k_norm.reshape(1, HEAD_DIM), cos, s_up, s_down)


def _global_attn_kernel(bound_ref, q_ref, k_ref, vt_ref, o_ref, qt_ref, *scratch, n_key_tiles):
    tq = q_ref.shape[-1]
    tk = vt_ref.shape[-1]
    n_lanes = GROUP * tq
    rows = tk // ATTN_CHUNKS
    cols = n_lanes // ATTN_CHUNKS
    n = ATTN_SLOTS
    s_bufs, p_bufs, a_bufs, c_bufs = (scratch[i * n:(i + 1) * n] for i in range(4))
    m_ref, l_ref, acc_ref = scratch[4 * n:]
    for g in range(GROUP):
        qt_ref[:, g * tq:(g + 1) * tq] = q_ref[g]
    bounded = bound_ref[0] <= SAFE_EXP2_RANGE
    l_ref[...] = jnp.zeros(l_ref.shape, F32)
    acc_ref[...] = jnp.zeros(acc_ref.shape, F32)

    def fold(x, op):
        return functools.reduce(op, [x[r:r + SUBLANES] for r in range(0, x.shape[0], SUBLANES)])

    def rounds(first, last, one_step, per_round=n):
        n_rounds = (last - first) // per_round

        def round_(i, carry):
            for u in range(per_round):
                one_step(first + per_round * i + u, first + u)
            return carry

        if n_rounds > 1:
            lax.fori_loop(0, n_rounds, round_, 0)
        elif n_rounds == 1:
            round_(0, 0)
        for t in range(first + n_rounds * per_round, last):
            one_step(t, t)

    def bounded_step(qk, pv):
        if qk is not None:
            l_part = l_ref[...]
        for c in range(ATTN_CHUNKS):
            rs = slice(c * rows, (c + 1) * rows)
            cs = slice(c * cols, (c + 1) * cols)
            if qk is not None:
                j, slot = qk
                kc = k_ref[pl.ds(pl.multiple_of(j * tk + c * rows, rows), rows), :]
                p = jnp.exp2(jnp.dot(kc, qt_ref[...], preferred_element_type=F32))
                l_part = l_part + fold(p, jnp.add)
                p_bufs[slot][rs, :] = p.astype(BF16)
            if pv is not None:
                j, slot = pv
                acc_ref[:, cs] += jnp.dot(vt_ref[j], p_bufs[slot][:, cs],
                                          preferred_element_type=F32)
        if qk is not None:
            l_ref[...] = l_part

    @pl.when(bounded)
    def _():
        bounded_step((0, 0), None)
        rounds(0, n_key_tiles - 1,
               lambda t, ts: bounded_step((t + 1, (ts + 1) % n), (t, ts % n)),
               per_round=n * ATTN_ROUND_UNROLL)
        bounded_step(None, (n_key_tiles - 1, (n_key_tiles - 1) % n))

    def step(qk, sm, pv):
        if sm is not None:
            m_prev = m_ref[...]
            m_new = jnp.maximum(m_prev, c_bufs[sm][...])
            alpha = jnp.exp2(m_prev - m_new)
            a_bufs[sm][...] = alpha
            m_ref[...] = m_new
            l_part = alpha * l_ref[...]
        if qk is not None:
            col_max = jnp.full((SUBLANES, n_lanes), -jnp.inf, F32)
        for c in range(ATTN_CHUNKS):
            rs = slice(c * rows, (c + 1) * rows)
            cs = slice(c * cols, (c + 1) * cols)
            if qk is not None:
                j, slot = qk
                kc = k_ref[pl.ds(pl.multiple_of(j * tk + c * rows, rows), rows), :]
                sc = jnp.dot(kc, qt_ref[...], preferred_element_type=F32)
                s_bufs[slot][rs, :] = sc
                col_max = jnp.maximum(col_max, fold(sc, jnp.maximum))
            if sm is not None:
                p = jnp.exp2(s_bufs[sm][rs, :] - m_new)
                l_part = l_part + fold(p, jnp.add)
                p_bufs[sm][rs, :] = p.astype(BF16)
            if pv is not None:
                j, slot = pv
                acc_ref[:, cs] = a_bufs[slot][:, cs] * acc_ref[:, cs] + jnp.dot(
                    vt_ref[j], p_bufs[slot][:, cs], preferred_element_type=F32)
        if qk is not None:
            c_bufs[qk[1]][...] = jnp.max(col_max, axis=0, keepdims=True)
        if sm is not None:
            l_ref[...] = l_part

    @pl.when(jnp.logical_not(bounded))
    def _():
        m_ref[...] = jnp.full(m_ref.shape, -jnp.inf, F32)
        step((0, 0), None, None)
        step((1, 1 % n), 0, None)
        rounds(1, n_key_tiles - 1,
               lambda t, ts: step((t + 1, (ts + 1) % n), ts % n, (t - 1, (ts - 1) % n)))
        step(None, (n_key_tiles - 1) % n, (n_key_tiles - 2, (n_key_tiles - 2) % n))
        step(None, None, (n_key_tiles - 1, (n_key_tiles - 1) % n))

    o = acc_ref[...] / jnp.sum(l_ref[...], axis=0, keepdims=True)
    for g in range(GROUP):
        o_ref[:, g * HEAD_DIM:(g + 1) * HEAD_DIM] = o[:, g * tq:(g + 1) * tq].T.astype(BF16)


def _score_bound(q_norm, k_norm):
    return (HEAD_DIM * SCALE * LOG2E * jnp.max(jnp.abs(q_norm)) * jnp.max(jnp.abs(k_norm))
            ).astype(F32).reshape(1)


def _global_attention(score_bound, qat, ka, vat, tok0, seq_len, n_seq):
    tq, tk = ATTN_Q_TILE, TOKEN_TILE
    q_tiles = seq_len // tq
    k_tiles = seq_len // tk
    assert tok0 % seq_len == 0 and k_tiles >= 2
    q0 = tok0 // tq
    s0 = tok0 // seq_len
    n_lanes = GROUP * tq
    return pl.pallas_call(
        functools.partial(_global_attn_kernel, n_key_tiles=k_tiles),
        grid=(n_seq, A_KV_HEADS, q_tiles),
        in_specs=[
            pl.BlockSpec(memory_space=pltpu.SMEM),
            pl.BlockSpec((GROUP, HEAD_DIM, tq), lambda b, h, i: (h, 0, q0 + b * q_tiles + i)),
            pl.BlockSpec((seq_len, HEAD_DIM), lambda b, h, i: (s0 + b, h)),
            pl.BlockSpec((None, k_tiles, HEAD_DIM, tk), lambda b, h, i: (h, s0 + b, 0, 0)),
        ],
        out_specs=pl.BlockSpec((tq, GROUP * HEAD_DIM), lambda b, h, i: (b * q_tiles + i, h)),
        out_shape=jax.ShapeDtypeStruct((n_seq * seq_len, A_WIDTH), BF16),
        scratch_shapes=[
            pltpu.VMEM((HEAD_DIM, n_lanes), BF16),
            *[pltpu.VMEM((tk, n_lanes), F32)] * ATTN_SLOTS,
            *[pltpu.VMEM((tk, n_lanes), BF16)] * ATTN_SLOTS,
            *[pltpu.VMEM((1, n_lanes), F32)] * ATTN_SLOTS,
            *[pltpu.VMEM((1, n_lanes), F32)] * ATTN_SLOTS,
            pltpu.VMEM((1, n_lanes), F32),
            pltpu.VMEM((SUBLANES, n_lanes), F32),
            pltpu.VMEM((HEAD_DIM, n_lanes), F32),
        ],
        compiler_params=_params("arbitrary", "arbitrary", "arbitrary"),
        name="global_attention",
    )(score_bound, qat, ka, vat)


def _mix_kernel(*refs, a_starts, x_starts, seq_starts, seq_ends):
    a_refs, refs = refs[:len(a_starts)], refs[len(a_starts):]
    x_refs, refs = refs[:len(x_starts)], refs[len(x_starts):]
    (sink_ref, qt_ref, kp_ref, kc_ref, kn_ref, vtp_ref, vtc_ref, vtn_ref, bias_ref,
     mod_ref, ga_ref, gb_ref, w_ref, gm_ref, o_ref, h_ref, ob_ref) = refs
    i = pl.program_id(0)

    mix_a = (_rms(_read_parts(a_refs, a_starts).astype(F32)) * ga_ref[...]).astype(BF16)
    x_in = _read_parts(x_refs, x_starts)
    gate = mod_ref[MOD_GATE_MIX:MOD_GATE_MIX + 1, :]
    n_chains = B_KV_HEADS * (qt_ref.shape[-1] // Q_BLOCK)
    a_cols = D_MODEL // n_chains

    def project_a(c):
        cs = slice(c * a_cols, (c + 1) * a_cols)
        o_ref[:, cs] = x_in[:, cs] + gate[:, cs] * jnp.dot(
            mix_a, w_ref[:A_WIDTH, cs], preferred_element_type=F32)

    sub_blocks = qt_ref.shape[-1] // Q_BLOCK
    n_lanes = GROUP * Q_BLOCK
    lane_group = lax.broadcasted_iota(jnp.int32, (1, n_lanes), 1) // Q_BLOCK
    neg_inf = jnp.full((WINDOW, n_lanes), -jnp.inf, F32)

    def fold(x, op):
        return functools.reduce(op, [x[r:r + SUBLANES] for r in range(0, x.shape[0], SUBLANES)])

    for h in range(B_KV_HEADS):
        hs = slice(h * HEAD_DIM, (h + 1) * HEAD_DIM)
        k_all = jnp.concatenate([kp_ref[:, hs], kc_ref[:, hs], kn_ref[:, hs]], axis=0)
        vt_all = jnp.concatenate([vtp_ref[h], vtc_ref[h], vtn_ref[h]], axis=1)
        sink = jnp.full((1, n_lanes), sink_ref[h * GROUP] * LOG2E, F32)
        for g in range(1, GROUP):
            sink = jnp.where(lane_group == g, sink_ref[h * GROUP + g] * LOG2E, sink)
        for u in range(sub_blocks):
            project_a(h * sub_blocks + u)
            blk = i * sub_blocks + u
            first = functools.reduce(jnp.logical_or, [blk == s for s in seq_starts])
            last = functools.reduce(jnp.logical_or, [blk == e - 1 for e in seq_ends])
            qs = slice(u * Q_BLOCK, (u + 1) * Q_BLOCK)
            qt = jnp.concatenate([qt_ref[h * GROUP + g, :, qs] for g in range(GROUP)], axis=1)
            kw = k_all[u * Q_BLOCK:u * Q_BLOCK + BAND]
            s = jnp.dot(kw, qt, preferred_element_type=F32) + bias_ref[h]
            s_prev = jnp.where(first, neg_inf, s[:WINDOW])
            s_mid = s[WINDOW:WINDOW + Q_BLOCK]
            s_next = jnp.where(last, neg_inf, s[WINDOW + Q_BLOCK:])
            col_max = functools.reduce(
                jnp.maximum, [fold(v, jnp.maximum) for v in (s_prev, s_mid, s_next)])
            m = jnp.maximum(jnp.max(col_max, axis=0, keepdims=True), sink)
            e = [jnp.exp2(v - m) for v in (s_prev, s_mid, s_next)]
            col_sum = functools.reduce(jnp.add, [fold(v, jnp.add) for v in e])
            den = jnp.sum(col_sum, axis=0, keepdims=True) + jnp.exp2(sink - m)
            p = jnp.concatenate([v.astype(BF16) for v in e], axis=0)
            ot = jnp.dot(vt_all[:, u * Q_BLOCK:u * Q_BLOCK + BAND], p,
                         preferred_element_type=F32) / den
            for g in range(GROUP):
                ob_ref[qs, (h * GROUP + g) * HEAD_DIM:(h * GROUP + g + 1) * HEAD_DIM] = (
                    ot[:, g * Q_BLOCK:(g + 1) * Q_BLOCK].T.astype(BF16))

    mix_b = (_rms(ob_ref[...].astype(F32)) * gb_ref[...]).astype(BF16)
    x = o_ref[...] + gate * jnp.dot(mix_b, w_ref[A_WIDTH:, :], preferred_element_type=F32)
    o_ref[...] = x
    h_ref[...] = (_rms(x) * gm_ref[...] * (1.0 + mod_ref[MOD_SCALE_MLP:MOD_SCALE_MLP + 1, :])
                  + mod_ref[MOD_SHIFT_MLP:MOD_SHIFT_MLP + 1, :]).astype(BF16)


def _mix_tail(out_a_parts, x_parts, qbt, kb, vbt, bias, sink, seq_bounds, mod_rows, out_norm_a,
              out_norm_b, w_out, norm_mlp):
    n_tok = kb.shape[0]
    tm = TOKEN_TILE
    n_tiles = n_tok // tm
    sub = tm // Q_BLOCK
    n_blocks = n_tok // Q_BLOCK
    seg_tiles = n_tiles // mod_rows.shape[0]
    seq_starts = tuple(s // Q_BLOCK for s, _ in seq_bounds)
    seq_ends = tuple(e // Q_BLOCK for _, e in seq_bounds)
    row = lambda i: (i, 0)
    const = lambda i: (0, 0)
    prev_block = lambda i: jnp.maximum(i * sub - 1, 0)
    next_block = lambda i: jnp.minimum(i * sub + sub, n_blocks - 1)
    k_prev = pl.BlockSpec((Q_BLOCK, KV_WIDTH), lambda i: (prev_block(i), 0))
    k_next = pl.BlockSpec((Q_BLOCK, KV_WIDTH), lambda i: (next_block(i), 0))
    vt_cur = pl.BlockSpec((B_KV_HEADS, HEAD_DIM, tm), lambda i: (0, 0, i))
    vt_prev = pl.BlockSpec((B_KV_HEADS, HEAD_DIM, Q_BLOCK), lambda i: (0, 0, prev_block(i)))
    vt_next = pl.BlockSpec((B_KV_HEADS, HEAD_DIM, Q_BLOCK), lambda i: (0, 0, next_block(i)))
    return pl.pallas_call(
        functools.partial(_mix_kernel, a_starts=_part_starts(out_a_parts, tm),
                          x_starts=_part_starts(x_parts, tm), seq_starts=seq_starts,
                          seq_ends=seq_ends),
        grid=(n_tiles,),
        in_specs=[
            *_part_specs(out_a_parts, tm, A_WIDTH),
            *_part_specs(x_parts, tm, D_MODEL),
            pl.BlockSpec(memory_space=pltpu.SMEM),
            pl.BlockSpec((B_Q_HEADS, HEAD_DIM, tm), lambda i: (0, 0, i)),
            k_prev, pl.BlockSpec((tm, KV_WIDTH), row), k_next, vt_prev, vt_cur, vt_next,
            pl.BlockSpec((B_KV_HEADS, BAND, GROUP * Q_BLOCK), lambda i: (0, 0, 0)),
            pl.BlockSpec((None, N_MOD, D_MODEL), lambda i: (i // seg_tiles, 0, 0)),
            pl.BlockSpec((1, A_WIDTH), const),
            pl.BlockSpec((1, B_WIDTH), const),
            pl.BlockSpec((MIX_WIDTH, D_MODEL), const),
            pl.BlockSpec((1, D_MODEL), const),
        ],
        out_specs=[pl.BlockSpec((tm, D_MODEL), row), pl.BlockSpec((tm, D_MODEL), row)],
        out_shape=[jax.ShapeDtypeStruct((n_tok, D_MODEL), F32),
                   jax.ShapeDtypeStruct((n_tok, D_MODEL), BF16)],
        scratch_shapes=[pltpu.VMEM((tm, B_WIDTH), BF16)],
        compiler_params=_params("arbitrary"),
        name="mix_tail",
    )(*out_a_parts, *x_parts, sink, qbt, kb, kb, kb, vbt, vbt, vbt, bias, mod_rows,
      out_norm_a.reshape(1, A_WIDTH), out_norm_b.reshape(1, B_WIDTH), w_out,
      norm_mlp.reshape(1, D_MODEL))


def _mlp_kernel(x_ref, h_ref, mod_ref, wu_ref, wd_ref, gf_ref, o_ref, *, final_norm):
    k = pl.program_id(1)

    @pl.when(k == 0)
    def _():
        o_ref[...] = jnp.zeros(o_ref.shape, F32)

    u = jnp.maximum(jnp.dot(h_ref[...], wu_ref[...], preferred_element_type=F32), 0.0)
    o_ref[...] += jnp.dot((u * u).astype(BF16), wd_ref[...], preferred_element_type=F32)

    @pl.when(k == pl.num_programs(1) - 1)
    def _():
        y = x_ref[...] + mod_ref[MOD_GATE_MLP:MOD_GATE_MLP + 1, :] * o_ref[...]
        if final_norm:
            y = _rms(y) * gf_ref[...]
        o_ref[...] = y


def _mlp(x, h, mod_rows, w_up, w_down, norm_final, final_norm, tok0=0, n_out=None):
    n_tok = x.shape[0]
    n_out = n_tok if n_out is None else n_out
    tm, tf = MLP_TOKEN_TILE, FF_TILE
    seg = n_tok // mod_rows.shape[0]
    tm = min(tm, seg)
    assert tok0 % tm == 0 and n_out % tm == 0
    t0, seg_tiles = tok0 // tm, seg // tm
    return pl.pallas_call(
        functools.partial(_mlp_kernel, final_norm=final_norm),
        grid=(n_out // tm, D_FF // tf),
        in_specs=[
            pl.BlockSpec((tm, D_MODEL), lambda i, k: (t0 + i, 0)),
            pl.BlockSpec((tm, D_MODEL), lambda i, k: (t0 + i, 0)),
            pl.BlockSpec((None, N_MOD, D_MODEL), lambda i, k: ((t0 + i) // seg_tiles, 0, 0)),
            pl.BlockSpec((D_MODEL, tf), lambda i, k: (0, k)),
            pl.BlockSpec((tf, D_MODEL), lambda i, k: (k, 0)),
            pl.BlockSpec((1, D_MODEL), lambda i, k: (0, 0)),
        ],
        out_specs=pl.BlockSpec((tm, D_MODEL), lambda i, k: (i, 0)),
        out_shape=jax.ShapeDtypeStruct((n_out, D_MODEL), F32),
        compiler_params=_params("arbitrary", "arbitrary"),
        name="mlp",
    )(x, h, mod_rows, w_up, w_down, norm_final.reshape(1, D_MODEL))


def kernel(x_prompt, x_sample, c_prompt, c_sample, w_mod, b_mod, norm_attn, w_in, q_norm, k_norm,
           sink, out_norm_a, out_norm_b, w_out, norm_mlp, w_up, w_down, rel_bias, norm_final):
    n_prompt, t_prompt, _ = x_prompt.shape
    n_sample, t_sample, _ = x_sample.shape
    tok_prompt = n_prompt * t_prompt
    n_tok = tok_prompt + n_sample * t_sample
    seg = math.gcd(t_prompt, t_sample)
    seg_row = ([b for b in range(n_prompt) for _ in range(t_prompt // seg)]
               + [n_prompt + b for b in range(n_sample) for _ in range(t_sample // seg)])
    seq_bounds = ([(b * t_prompt, (b + 1) * t_prompt) for b in range(n_prompt)]
                  + [(tok_prompt + b * t_sample, tok_prompt + (b + 1) * t_sample)
                     for b in range(n_sample)])

    x_parts = (x_prompt.reshape(tok_prompt, D_MODEL), x_sample.reshape(n_tok - tok_prompt, D_MODEL))
    c_rows = jnp.concatenate([c_prompt, c_sample], axis=0)
    c_rows = jnp.pad(c_rows, ((0, MOD_ROWS - c_rows.shape[0]), (0, 0)))
    mod = _modulation(c_rows, w_mod, b_mod)
    mod = mod.reshape(DEPTH, MOD_ROWS, N_MOD, D_MODEL)[:, jnp.asarray(seg_row)]

    rope = _rope_tables(max(t_prompt, t_sample))
    tiles_prompt, tiles_sample = t_prompt // TOKEN_TILE, t_sample // TOKEN_TILE
    n_tiles_prompt = tok_prompt // TOKEN_TILE

    def rope_block(i):
        return jnp.where(i < n_tiles_prompt, i % tiles_prompt, (i - n_tiles_prompt) % tiles_sample)

    bias = _band_bias(rel_bias)
    w_in_b, w_out_b = w_in.astype(BF16), w_out.astype(BF16)
    w_up_b, w_down_b = w_up.astype(BF16), w_down.astype(BF16)

    for l in range(DEPTH):
        qat, ka, vat, qbt, kb, vbt = _in_projection(x_parts, mod[l], norm_attn[l], w_in_b[l],
                                                    q_norm[l], k_norm[l], rope, rope_block)
        bound = _score_bound(q_norm[l], k_norm[l])
        out_a_parts = (_global_attention(bound, qat, ka, vat, 0, t_prompt, n_prompt),
                       _global_attention(bound, qat, ka, vat, tok_prompt, t_sample, n_sample))
        x, h = _mix_tail(out_a_parts, x_parts, qbt, kb, vbt, bias, sink[l], seq_bounds, mod[l],
                         out_norm_a[l], out_norm_b[l], w_out_b[l], norm_mlp[l])
        mlp = functools.partial(_mlp, x, h, mod[l], w_up_b[l], w_down_b[l], norm_final)
        if l < DEPTH - 1:
            x_parts = (mlp(False),)
    y_prompt = mlp(True, 0, tok_prompt).reshape(n_prompt, t_prompt, D_MODEL)
    y_sample = mlp(True, tok_prompt, n_tok - tok_prompt).reshape(n_sample, t_sample, D_MODEL)
    return y_prompt, y_sample
```

```python
import functools
import math

import jax
import jax.numpy as jnp
from jax import lax
from jax.experimental import pallas as pl
from jax.experimental.pallas import tpu as pltpu

D_MODEL = 2048
DEPTH = 4
HEAD_DIM = 128
A_Q_HEADS = 8
A_KV_HEADS = 2
B_Q_HEADS = 8
B_KV_HEADS = 2
GROUP = A_Q_HEADS // A_KV_HEADS
A_WIDTH = A_Q_HEADS * HEAD_DIM
B_WIDTH = B_Q_HEADS * HEAD_DIM
MIX_WIDTH = A_WIDTH + B_WIDTH
KV_WIDTH = A_KV_HEADS * HEAD_DIM
IN_WIDTH = A_WIDTH + 2 * KV_WIDTH + B_WIDTH + 2 * KV_WIDTH
D_FF = 4 * D_MODEL
Q_BLOCK = 128
WINDOW = 128
BAND = Q_BLOCK + 2 * WINDOW
NUM_BUCKETS = 32
MAX_DISTANCE = 128
GRID_W = 64
ROPE_THETA = 10000.0
EPS = 1e-6
SCALE = HEAD_DIM ** -0.5
LOG2E = math.log2(math.e)

OFF_QA = 0
OFF_KA = OFF_QA + A_WIDTH
OFF_VA = OFF_KA + KV_WIDTH
OFF_QB = OFF_VA + KV_WIDTH
OFF_KB = OFF_QB + B_WIDTH
OFF_VB = OFF_KB + KV_WIDTH

MOD_ROWS = 8
N_MOD = 6
MOD_SHIFT_MIX, MOD_SCALE_MIX, MOD_GATE_MIX, MOD_SHIFT_MLP, MOD_SCALE_MLP, MOD_GATE_MLP = range(N_MOD)
TOKEN_TILE = 512
MLP_TOKEN_TILE = 1024
FF_TILE = 512
ATTN_Q_TILE = 256
ATTN_CHUNKS = 1
ATTN_SLOTS = 3
ATTN_ROUND_UNROLL = 11
SAFE_EXP2_RANGE = 60.0
SUBLANES = 8
VMEM_LIMIT = 60 * 1024 * 1024

F32 = jnp.float32
BF16 = jnp.bfloat16


def _params(*semantics):
    return pltpu.CompilerParams(dimension_semantics=semantics, vmem_limit_bytes=VMEM_LIMIT)


def _rms(x):
    return x * lax.rsqrt(jnp.mean(x * x, axis=-1, keepdims=True) + EPS)


def _part_starts(parts, tm):
    starts, start = [], 0
    for p in parts:
        starts.append(start)
        start += p.shape[0] // tm
    return tuple(starts)


def _part_specs(parts, tm, width):
    specs = []
    for p, start in zip(parts, _part_starts(parts, tm)):
        n = p.shape[0] // tm
        specs.append(pl.BlockSpec((tm, width),
                                  lambda i, start=start, n=n: (jnp.clip(i - start, 0, n - 1), 0)))
    return specs


def _read_parts(refs, starts):
    i = pl.program_id(0)
    x = refs[0][...]
    for ref, start in zip(refs[1:], starts[1:]):
        x = jnp.where(i >= start, ref[...], x)
    return x


def _mod_kernel(c_ref, w_ref, b_ref, o_ref):
    c = c_ref[...]
    a = (c * jax.nn.sigmoid(c)).astype(BF16)
    o_ref[...] = jnp.dot(a, w_ref[...].astype(BF16), preferred_element_type=F32) + b_ref[...]


def _modulation(c_rows, w_mod, b_mod):
    tn = D_MODEL
    return pl.pallas_call(
        _mod_kernel,
        grid=(DEPTH, N_MOD * D_MODEL // tn),
        in_specs=[
            pl.BlockSpec((MOD_ROWS, D_MODEL), lambda l, n: (0, 0)),
            pl.BlockSpec((None, D_MODEL, tn), lambda l, n: (l, 0, n)),
            pl.BlockSpec((None, 1, tn), lambda l, n: (l, 0, n)),
        ],
        out_specs=pl.BlockSpec((None, MOD_ROWS, tn), lambda l, n: (l, 0, n)),
        out_shape=jax.ShapeDtypeStruct((DEPTH, MOD_ROWS, N_MOD * D_MODEL), F32),
        compiler_params=_params("arbitrary", "arbitrary"),
        name="modulation",
    )(c_rows, w_mod, b_mod.reshape(DEPTH, 1, N_MOD * D_MODEL))


def _bias_kernel(bucket_ref, rel_bias_ref, o_ref):
    bucket = bucket_ref[...]
    in_band = bucket >= 0
    for h in range(B_Q_HEADS):
        acc = jnp.zeros((BAND, Q_BLOCK), F32)
        for b in range(NUM_BUCKETS):
            acc = jnp.where(bucket == b, rel_bias_ref[b, h], acc)
        g = h % GROUP
        o_ref[h // GROUP, :, g * Q_BLOCK:(g + 1) * Q_BLOCK] = jnp.where(in_band, acc * LOG2E, -jnp.inf)


def _t5_bucket(rel):
    half = NUM_BUCKETS // 2
    max_exact = half // 2
    n = jnp.abs(rel)
    nf = jnp.maximum(n, 1).astype(F32)
    large = max_exact + (jnp.log(nf / max_exact) / math.log(MAX_DISTANCE / max_exact)
                         * (half - max_exact)).astype(jnp.int32)
    large = jnp.minimum(large, half - 1)
    return jnp.where(rel > 0, half, 0) + jnp.where(n < max_exact, n, large)


def _band_bias(rel_bias):
    rel = jnp.arange(BAND)[:, None] - WINDOW - jnp.arange(Q_BLOCK)[None, :]
    bucket = jnp.where(jnp.abs(rel) <= WINDOW, _t5_bucket(rel), -1).astype(jnp.int32)
    return pl.pallas_call(
        _bias_kernel,
        in_specs=[
            pl.BlockSpec(memory_space=pltpu.VMEM),
            pl.BlockSpec(memory_space=pltpu.SMEM),
        ],
        out_specs=pl.BlockSpec(memory_space=pltpu.VMEM),
        out_shape=jax.ShapeDtypeStruct((B_KV_HEADS, BAND, GROUP * Q_BLOCK), F32),
        name="band_bias",
    )(bucket, rel_bias)


def _rope_tables(t_len):
    rows = t_len // GRID_W
    row_ids = jnp.repeat(jnp.arange(rows, dtype=F32), GRID_W)
    col_ids = jnp.tile(jnp.arange(GRID_W, dtype=F32), rows)
    half = HEAD_DIM // 2
    inv_freq = ROPE_THETA ** (-jnp.arange(0, half, 2, dtype=F32) / half)
    ang_r = row_ids[:, None] * inv_freq[None, :]
    ang_c = col_ids[:, None] * inv_freq[None, :]
    ang = jnp.concatenate([ang_r, ang_r, ang_c, ang_c], axis=-1)
    cos, sin = jnp.cos(ang), jnp.sin(ang)
    quarter = (jnp.arange(HEAD_DIM) // (HEAD_DIM // 4))[None, :]
    sin_up = jnp.where(quarter % 2 == 0, -sin, 0.0)
    sin_down = jnp.where(quarter % 2 == 1, sin, 0.0)
    return cos, sin_up, sin_down


def _inproj_kernel(*refs, x_starts):
    x_refs, refs = refs[:len(x_starts)], refs[len(x_starts):]
    (mod_ref, g_ref, w_ref, qn_ref, kn_ref, cos_ref, su_ref, sd_ref,
     qat_ref, ka_ref, vat_ref, qbt_ref, kb_ref, vbt_ref) = refs
    x = _read_parts(x_refs, x_starts)
    h = (_rms(x) * g_ref[...] * (1.0 + mod_ref[MOD_SCALE_MIX:MOD_SCALE_MIX + 1, :])
         + mod_ref[MOD_SHIFT_MIX:MOD_SHIFT_MIX + 1, :]).astype(BF16)

    def proj(lo, width):
        return jnp.dot(h, w_ref[:, lo:lo + width], preferred_element_type=F32)

    cos, s_up, s_down = cos_ref[...], su_ref[...], sd_ref[...]
    quarter = HEAD_DIM // 4

    def norm_rope(z, gain):
        zn = _rms(z) * gain
        return (zn * cos + pltpu.roll(zn, HEAD_DIM - quarter, 1) * s_up
                + pltpu.roll(zn, quarter, 1) * s_down)

    qa = proj(OFF_QA, A_WIDTH)
    for hd in range(A_Q_HEADS):
        sl = slice(hd * HEAD_DIM, (hd + 1) * HEAD_DIM)
        qat_ref[hd] = (norm_rope(qa[:, sl], qn_ref[...]) * (SCALE * LOG2E)).T.astype(BF16)
    ka = proj(OFF_KA, KV_WIDTH)
    for hd in range(A_KV_HEADS):
        sl = slice(hd * HEAD_DIM, (hd + 1) * HEAD_DIM)
        ka_ref[:, sl] = norm_rope(ka[:, sl], kn_ref[...]).astype(BF16)
    va = proj(OFF_VA, KV_WIDTH)
    for hd in range(A_KV_HEADS):
        sl = slice(hd * HEAD_DIM, (hd + 1) * HEAD_DIM)
        vat_ref[hd, 0] = va[:, sl].T.astype(BF16)
    qb = proj(OFF_QB, B_WIDTH)
    for hd in range(B_Q_HEADS):
        sl = slice(hd * HEAD_DIM, (hd + 1) * HEAD_DIM)
        qbt_ref[hd] = (qb[:, sl] * (SCALE * LOG2E)).T.astype(BF16)
    kb_ref[...] = proj(OFF_KB, KV_WIDTH).astype(BF16)
    vb = proj(OFF_VB, KV_WIDTH)
    for hd in range(B_KV_HEADS):
        sl = slice(hd * HEAD_DIM, (hd + 1) * HEAD_DIM)
        vbt_ref[hd] = vb[:, sl].T.astype(BF16)


def _in_projection(x_parts, mod_rows, norm_attn, w_in, layer, q_norm, k_norm, rope, rope_block):
    n_tok = sum(p.shape[0] for p in x_parts)
    tm = TOKEN_TILE
    n_tiles = n_tok // tm
    per_seg = mod_rows.shape[0]
    seg_tiles = n_tiles // per_seg
    cos, s_up, s_down = rope
    row = lambda i: (i, 0)
    const = lambda i: (0, 0)
    rope_spec = pl.BlockSpec((tm, HEAD_DIM), lambda i: (rope_block(i), 0))
    return pl.pallas_call(
        functools.partial(_inproj_kernel, x_starts=_part_starts(x_parts, tm)),
        grid=(n_tiles,),
        in_specs=[
            *_part_specs(x_parts, tm, D_MODEL),
            pl.BlockSpec((None, N_MOD, D_MODEL), lambda i: (i // seg_tiles, 0, 0)),
            pl.BlockSpec((1, D_MODEL), const),
            pl.BlockSpec((None, D_MODEL, IN_WIDTH), lambda i: (layer, 0, 0)),
            pl.BlockSpec((1, HEAD_DIM), const),
            pl.BlockSpec((1, HEAD_DIM), const),
            rope_spec, rope_spec, rope_spec,
        ],
        out_specs=[
            pl.BlockSpec((A_Q_HEADS, HEAD_DIM, tm), lambda i: (0, 0, i)),
            pl.BlockSpec((tm, KV_WIDTH), row),
            pl.BlockSpec((A_KV_HEADS, 1, HEAD_DIM, tm), lambda i: (0, i, 0, 0)),
            pl.BlockSpec((B_Q_HEADS, HEAD_DIM, tm), lambda i: (0, 0, i)),
            pl.BlockSpec((tm, KV_WIDTH), row),
            pl.BlockSpec((B_KV_HEADS, HEAD_DIM, tm), lambda i: (0, 0, i)),
        ],
        out_shape=[
            jax.ShapeDtypeStruct((A_Q_HEADS, HEAD_DIM, n_tok), BF16),
            jax.ShapeDtypeStruct((n_tok, KV_WIDTH), BF16),
            jax.ShapeDtypeStruct((A_KV_HEADS, n_tiles, HEAD_DIM, tm), BF16),
            jax.ShapeDtypeStruct((B_Q_HEADS, HEAD_DIM, n_tok), BF16),
            jax.ShapeDtypeStruct((n_tok, KV_WIDTH), BF16),
            jax.ShapeDtypeStruct((B_KV_HEADS, HEAD_DIM, n_tok), BF16),
        ],
        compiler_params=_params("arbitrary"),
        name="in_projection",
    )(*x_parts, mod_rows, norm_attn.reshape(1, D_MODEL), w_in, q_norm.reshape(1, HEAD_DIM),
      k_norm.reshape(1, HEAD_DIM), cos, s_up, s_down)


def _global_attn_kernel(bound_ref, q_ref, k_ref, vt_ref, o_ref, qt_ref, *scratch, n_key_tiles):
    tq = q_ref.shape[-1]
    tk = vt_ref.shape[-1]
    n_lanes = GROUP * tq
    rows = tk // ATTN_CHUNKS
    cols = n_lanes // ATTN_CHUNKS
    n = ATTN_SLOTS
    s_bufs, p_bufs, a_bufs, c_bufs = (scratch[i * n:(i + 1) * n] for i in range(4))
    m_ref, l_ref, acc_ref = scratch[4 * n:]
    for g in range(GROUP):
        qt_ref[:, g * tq:(g + 1) * tq] = q_ref[g]
    bounded = bound_ref[0] <= SAFE_EXP2_RANGE
    l_ref[...] = jnp.zeros(l_ref.shape, F32)
    acc_ref[...] = jnp.zeros(acc_ref.shape, F32)

    def fold(x, op):
        return functools.reduce(op, [x[r:r + SUBLANES] for r in range(0, x.shape[0], SUBLANES)])

    def rounds(first, last, one_step, per_round=n):
        n_rounds = (last - first) // per_round

        def round_(i, carry):
            for u in range(per_round):
                one_step(first + per_round * i + u, first + u)
            return carry

        if n_rounds > 1:
            lax.fori_loop(0, n_rounds, round_, 0)
        elif n_rounds == 1:
            round_(0, 0)
        for t in range(first + n_rounds * per_round, last):
            one_step(t, t)

    def bounded_step(qk, pv):
        if qk is not None:
            l_part = l_ref[...]
        for c in range(ATTN_CHUNKS):
            rs = slice(c * rows, (c + 1) * rows)
            cs = slice(c * cols, (c + 1) * cols)
            if qk is not None:
                j, slot = qk
                kc = k_ref[pl.ds(pl.multiple_of(j * tk + c * rows, rows), rows), :]
                p = jnp.exp2(jnp.dot(kc, qt_ref[...], preferred_element_type=F32))
                l_part = l_part + fold(p, jnp.add)
                p_bufs[slot][rs, :] = p.astype(BF16)
            if pv is not None:
                j, slot = pv
                acc_ref[:, cs] += jnp.dot(vt_ref[j], p_bufs[slot][:, cs],
                                          preferred_element_type=F32)
        if qk is not None:
            l_ref[...] = l_part

    @pl.when(bounded)
    def _():
        bounded_step((0, 0), None)
        rounds(0, n_key_tiles - 1,
               lambda t, ts: bounded_step((t + 1, (ts + 1) % n), (t, ts % n)),
               per_round=n * ATTN_ROUND_UNROLL)
        bounded_step(None, (n_key_tiles - 1, (n_key_tiles - 1) % n))

    def step(qk, sm, pv):
        if sm is not None:
            m_prev = m_ref[...]
            m_new = jnp.maximum(m_prev, c_bufs[sm][...])
            alpha = jnp.exp2(m_prev - m_new)
            a_bufs[sm][...] = alpha
            m_ref[...] = m_new
            l_part = alpha * l_ref[...]
        if qk is not None:
            col_max = jnp.full((SUBLANES, n_lanes), -jnp.inf, F32)
        for c in range(ATTN_CHUNKS):
            rs = slice(c * rows, (c + 1) * rows)
            cs = slice(c * cols, (c + 1) * cols)
            if qk is not None:
                j, slot = qk
                kc = k_ref[pl.ds(pl.multiple_of(j * tk + c * rows, rows), rows), :]
                sc = jnp.dot(kc, qt_ref[...], preferred_element_type=F32)
                s_bufs[slot][rs, :] = sc
                col_max = jnp.maximum(col_max, fold(sc, jnp.maximum))
            if sm is not None:
                p = jnp.exp2(s_bufs[sm][rs, :] - m_new)
                l_part = l_part + fold(p, jnp.add)
                p_bufs[sm][rs, :] = p.astype(BF16)
            if pv is not None:
                j, slot = pv
                acc_ref[:, cs] = a_bufs[slot][:, cs] * acc_ref[:, cs] + jnp.dot(
                    vt_ref[j], p_bufs[slot][:, cs], preferred_element_type=F32)
        if qk is not None:
            c_bufs[qk[1]][...] = jnp.max(col_max, axis=0, keepdims=True)
        if sm is not None:
            l_ref[...] = l_part

    @pl.when(jnp.logical_not(bounded))
    def _():
        m_ref[...] = jnp.full(m_ref.shape, -jnp.inf, F32)
        step((0, 0), None, None)
        step((1, 1 % n), 0, None)
        rounds(1, n_key_tiles - 1,
               lambda t, ts: step((t + 1, (ts + 1) % n), ts % n, (t - 1, (ts - 1) % n)))
        step(None, (n_key_tiles - 1) % n, (n_key_tiles - 2, (n_key_tiles - 2) % n))
        step(None, None, (n_key_tiles - 1, (n_key_tiles - 1) % n))

    o = acc_ref[...] / jnp.sum(l_ref[...], axis=0, keepdims=True)
    for g in range(GROUP):
        o_ref[:, g * HEAD_DIM:(g + 1) * HEAD_DIM] = o[:, g * tq:(g + 1) * tq].T.astype(BF16)


def _score_bound(q_norm, k_norm):
    return (HEAD_DIM * SCALE * LOG2E * jnp.max(jnp.abs(q_norm)) * jnp.max(jnp.abs(k_norm))
            ).astype(F32).reshape(1)


def _global_attention(score_bound, qat, ka, vat, tok0, seq_len, n_seq):
    tq, tk = ATTN_Q_TILE, TOKEN_TILE
    q_tiles = seq_len // tq
    k_tiles = seq_len // tk
    assert tok0 % seq_len == 0 and k_tiles >= 2
    q0 = tok0 // tq
    s0 = tok0 // seq_len
    n_lanes = GROUP * tq
    return pl.pallas_call(
        functools.partial(_global_attn_kernel, n_key_tiles=k_tiles),
        grid=(n_seq, A_KV_HEADS, q_tiles),
        in_specs=[
            pl.BlockSpec(memory_space=pltpu.SMEM),
            pl.BlockSpec((GROUP, HEAD_DIM, tq), lambda b, h, i: (h, 0, q0 + b * q_tiles + i)),
            pl.BlockSpec((seq_len, HEAD_DIM), lambda b, h, i: (s0 + b, h)),
            pl.BlockSpec((None, k_tiles, HEAD_DIM, tk), lambda b, h, i: (h, s0 + b, 0, 0)),
        ],
        out_specs=pl.BlockSpec((tq, GROUP * HEAD_DIM), lambda b, h, i: (b * q_tiles + i, h)),
        out_shape=jax.ShapeDtypeStruct((n_seq * seq_len, A_WIDTH), BF16),
        scratch_shapes=[
            pltpu.VMEM((HEAD_DIM, n_lanes), BF16),
            *[pltpu.VMEM((tk, n_lanes), F32)] * ATTN_SLOTS,
            *[pltpu.VMEM((tk, n_lanes), BF16)] * ATTN_SLOTS,
            *[pltpu.VMEM((1, n_lanes), F32)] * ATTN_SLOTS,
            *[pltpu.VMEM((1, n_lanes), F32)] * ATTN_SLOTS,
            pltpu.VMEM((1, n_lanes), F32),
            pltpu.VMEM((SUBLANES, n_lanes), F32),
            pltpu.VMEM((HEAD_DIM, n_lanes), F32),
        ],
        compiler_params=_params("arbitrary", "arbitrary", "arbitrary"),
        name="global_attention",
    )(score_bound, qat, ka, vat)


def _mix_kernel(*refs, a_starts, x_starts, seq_starts, seq_ends):
    a_refs, refs = refs[:len(a_starts)], refs[len(a_starts):]
    x_refs, refs = refs[:len(x_starts)], refs[len(x_starts):]
    (sink_ref, qt_ref, kp_ref, kc_ref, kn_ref, vtp_ref, vtc_ref, vtn_ref, bias_ref,
     mod_ref, ga_ref, gb_ref, w_ref, gm_ref, o_ref, h_ref, ob_ref) = refs
    i = pl.program_id(0)

    mix_a = (_rms(_read_parts(a_refs, a_starts).astype(F32)) * ga_ref[...]).astype(BF16)
    x_in = _read_parts(x_refs, x_starts)
    gate = mod_ref[MOD_GATE_MIX:MOD_GATE_MIX + 1, :]
    n_chains = B_KV_HEADS * (qt_ref.shape[-1] // Q_BLOCK)
    a_cols = D_MODEL // n_chains

    def project_a(c):
        cs = slice(c * a_cols, (c + 1) * a_cols)
        o_ref[:, cs] = x_in[:, cs] + gate[:, cs] * jnp.dot(
            mix_a, w_ref[:A_WIDTH, cs], preferred_element_type=F32)

    sub_blocks = qt_ref.shape[-1] // Q_BLOCK
    n_lanes = GROUP * Q_BLOCK
    lane_group = lax.broadcasted_iota(jnp.int32, (1, n_lanes), 1) // Q_BLOCK
    neg_inf = jnp.full((WINDOW, n_lanes), -jnp.inf, F32)

    def fold(x, op):
        return functools.reduce(op, [x[r:r + SUBLANES] for r in range(0, x.shape[0], SUBLANES)])

    for h in range(B_KV_HEADS):
        hs = slice(h * HEAD_DIM, (h + 1) * HEAD_DIM)
        k_all = jnp.concatenate([kp_ref[:, hs], kc_ref[:, hs], kn_ref[:, hs]], axis=0)
        vt_all = jnp.concatenate([vtp_ref[h], vtc_ref[h], vtn_ref[h]], axis=1)
        sink = jnp.full((1, n_lanes), sink_ref[h * GROUP] * LOG2E, F32)
        for g in range(1, GROUP):
            sink = jnp.where(lane_group == g, sink_ref[h * GROUP + g] * LOG2E, sink)
        for u in range(sub_blocks):
            project_a(h * sub_blocks + u)
            blk = i * sub_blocks + u
            first = functools.reduce(jnp.logical_or, [blk == s for s in seq_starts])
            last = functools.reduce(jnp.logical_or, [blk == e - 1 for e in seq_ends])
            qs = slice(u * Q_BLOCK, (u + 1) * Q_BLOCK)
            qt = jnp.concatenate([qt_ref[h * GROUP + g, :, qs] for g in range(GROUP)], axis=1)
            kw = k_all[u * Q_BLOCK:u * Q_BLOCK + BAND]
            s = jnp.dot(kw, qt, preferred_element_type=F32) + bias_ref[h]
            s_prev = jnp.where(first, neg_inf, s[:WINDOW])
            s_mid = s[WINDOW:WINDOW + Q_BLOCK]
            s_next = jnp.where(last, neg_inf, s[WINDOW + Q_BLOCK:])
            col_max = functools.reduce(
                jnp.maximum, [fold(v, jnp.maximum) for v in (s_prev, s_mid, s_next)])
            m = jnp.maximum(jnp.max(col_max, axis=0, keepdims=True), sink)
            e = [jnp.exp2(v - m) for v in (s_prev, s_mid, s_next)]
            col_sum = functools.reduce(jnp.add, [fold(v, jnp.add) for v in e])
            den = jnp.sum(col_sum, axis=0, keepdims=True) + jnp.exp2(sink - m)
            p = jnp.concatenate([v.astype(BF16) for v in e], axis=0)
            ot = jnp.dot(vt_all[:, u * Q_BLOCK:u * Q_BLOCK + BAND], p,
                         preferred_element_type=F32) / den
            for g in range(GROUP):
                ob_ref[qs, (h * GROUP + g) * HEAD_DIM:(h * GROUP + g + 1) * HEAD_DIM] = (
                    ot[:, g * Q_BLOCK:(g + 1) * Q_BLOCK].T.astype(BF16))

    mix_b = (_rms(ob_ref[...].astype(F32)) * gb_ref[...]).astype(BF16)
    x = o_ref[...] + gate * jnp.dot(mix_b, w_ref[A_WIDTH:, :], preferred_element_type=F32)
    o_ref[...] = x
    h_ref[...] = (_rms(x) * gm_ref[...] * (1.0 + mod_ref[MOD_SCALE_MLP:MOD_SCALE_MLP + 1, :])
                  + mod_ref[MOD_SHIFT_MLP:MOD_SHIFT_MLP + 1, :]).astype(BF16)


def _mix_tail(out_a_parts, x_parts, qbt, kb, vbt, bias, sink, seq_bounds, mod_rows, out_norm_a,
              out_norm_b, w_out, layer, norm_mlp):
    n_tok = kb.shape[0]
    tm = TOKEN_TILE
    n_tiles = n_tok // tm
    sub = tm // Q_BLOCK
    n_blocks = n_tok // Q_BLOCK
    seg_tiles = n_tiles // mod_rows.shape[0]
    seq_starts = tuple(s // Q_BLOCK for s, _ in seq_bounds)
    seq_ends = tuple(e // Q_BLOCK for _, e in seq_bounds)
    row = lambda i: (i, 0)
    const = lambda i: (0, 0)
    prev_block = lambda i: jnp.maximum(i * sub - 1, 0)
    next_block = lambda i: jnp.minimum(i * sub + sub, n_blocks - 1)
    k_prev = pl.BlockSpec((Q_BLOCK, KV_WIDTH), lambda i: (prev_block(i), 0))
    k_next = pl.BlockSpec((Q_BLOCK, KV_WIDTH), lambda i: (next_block(i), 0))
    vt_cur = pl.BlockSpec((B_KV_HEADS, HEAD_DIM, tm), lambda i: (0, 0, i))
    vt_prev = pl.BlockSpec((B_KV_HEADS, HEAD_DIM, Q_BLOCK), lambda i: (0, 0, prev_block(i)))
    vt_next = pl.BlockSpec((B_KV_HEADS, HEAD_DIM, Q_BLOCK), lambda i: (0, 0, next_block(i)))
    return pl.pallas_call(
        functools.partial(_mix_kernel, a_starts=_part_starts(out_a_parts, tm),
                          x_starts=_part_starts(x_parts, tm), seq_starts=seq_starts,
                          seq_ends=seq_ends),
        grid=(n_tiles,),
        in_specs=[
            *_part_specs(out_a_parts, tm, A_WIDTH),
            *_part_specs(x_parts, tm, D_MODEL),
            pl.BlockSpec(memory_space=pltpu.SMEM),
            pl.BlockSpec((B_Q_HEADS, HEAD_DIM, tm), lambda i: (0, 0, i)),
            k_prev, pl.BlockSpec((tm, KV_WIDTH), row), k_next, vt_prev, vt_cur, vt_next,
            pl.BlockSpec((B_KV_HEADS, BAND, GROUP * Q_BLOCK), lambda i: (0, 0, 0)),
            pl.BlockSpec((None, N_MOD, D_MODEL), lambda i: (i // seg_tiles, 0, 0)),
            pl.BlockSpec((1, A_WIDTH), const),
            pl.BlockSpec((1, B_WIDTH), const),
            pl.BlockSpec((None, MIX_WIDTH, D_MODEL), lambda i: (layer, 0, 0)),
            pl.BlockSpec((1, D_MODEL), const),
        ],
        out_specs=[pl.BlockSpec((tm, D_MODEL), row), pl.BlockSpec((tm, D_MODEL), row)],
        out_shape=[jax.ShapeDtypeStruct((n_tok, D_MODEL), F32),
                   jax.ShapeDtypeStruct((n_tok, D_MODEL), BF16)],
        scratch_shapes=[pltpu.VMEM((tm, B_WIDTH), BF16)],
        compiler_params=_params("arbitrary"),
        name="mix_tail",
    )(*out_a_parts, *x_parts, sink, qbt, kb, kb, kb, vbt, vbt, vbt, bias, mod_rows,
      out_norm_a.reshape(1, A_WIDTH), out_norm_b.reshape(1, B_WIDTH), w_out,
      norm_mlp.reshape(1, D_MODEL))


def _mlp_kernel(x_ref, h_ref, mod_ref, wu_ref, wd_ref, gf_ref, o_ref, *, final_norm):
    k = pl.program_id(1)

    @pl.when(k == 0)
    def _():
        o_ref[...] = jnp.zeros(o_ref.shape, F32)

    u = jnp.maximum(jnp.dot(h_ref[...], wu_ref[...], preferred_element_type=F32), 0.0)
    o_ref[...] += jnp.dot((u * u).astype(BF16), wd_ref[...], preferred_element_type=F32)

    @pl.when(k == pl.num_programs(1) - 1)
    def _():
        y = x_ref[...] + mod_ref[MOD_GATE_MLP:MOD_GATE_MLP + 1, :] * o_ref[...]
        if final_norm:
            y = _rms(y) * gf_ref[...]
        o_ref[...] = y


def _mlp(x, h, mod_rows, w_up, w_down, layer, norm_final, final_norm, tok0=0, n_out=None):
    n_tok = x.shape[0]
    n_out = n_tok if n_out is None else n_out
    tm, tf = MLP_TOKEN_TILE, FF_TILE
    seg = n_tok // mod_rows.shape[0]
    tm = min(tm, seg)
    assert tok0 % tm == 0 and n_out % tm == 0
    t0, seg_tiles = tok0 // tm, seg // tm
    return pl.pallas_call(
        functools.partial(_mlp_kernel, final_norm=final_norm),
        grid=(n_out // tm, D_FF // tf),
        in_specs=[
            pl.BlockSpec((tm, D_MODEL), lambda i, k: (t0 + i, 0)),
            pl.BlockSpec((tm, D_MODEL), lambda i, k: (t0 + i, 0)),
            pl.BlockSpec((None, N_MOD, D_MODEL), lambda i, k: ((t0 + i) // seg_tiles, 0, 0)),
            pl.BlockSpec((None, D_MODEL, tf), lambda i, k: (layer, 0, k)),
            pl.BlockSpec((None, tf, D_MODEL), lambda i, k: (layer, k, 0)),
            pl.BlockSpec((1, D_MODEL), lambda i, k: (0, 0)),
        ],
        out_specs=pl.BlockSpec((tm, D_MODEL), lambda i, k: (i, 0)),
        out_shape=jax.ShapeDtypeStruct((n_out, D_MODEL), F32),
        compiler_params=_params("arbitrary", "arbitrary"),
        name="mlp",
    )(x, h, mod_rows, w_up, w_down, norm_final.reshape(1, D_MODEL))


def kernel(x_prompt, x_sample, c_prompt, c_sample, w_mod, b_mod, norm_attn, w_in, q_norm, k_norm,
           sink, out_norm_a, out_norm_b, w_out, norm_mlp, w_up, w_down, rel_bias, norm_final):
    n_prompt, t_prompt, _ = x_prompt.shape
    n_sample, t_sample, _ = x_sample.shape
    tok_prompt = n_prompt * t_prompt
    n_tok = tok_prompt + n_sample * t_sample
    seg = math.gcd(t_prompt, t_sample)
    seg_row = ([b for b in range(n_prompt) for _ in range(t_prompt // seg)]
               + [n_prompt + b for b in range(n_sample) for _ in range(t_sample // seg)])
    seq_bounds = ([(b * t_prompt, (b + 1) * t_prompt) for b in range(n_prompt)]
                  + [(tok_prompt + b * t_sample, tok_prompt + (b + 1) * t_sample)
                     for b in range(n_sample)])

    x_parts = (x_prompt.reshape(tok_prompt, D_MODEL), x_sample.reshape(n_tok - tok_prompt, D_MODEL))
    c_rows = jnp.concatenate([c_prompt, c_sample], axis=0)
    c_rows = jnp.pad(c_rows, ((0, MOD_ROWS - c_rows.shape[0]), (0, 0)))
    mod = _modulation(c_rows, w_mod, b_mod)
    mod = mod.reshape(DEPTH, MOD_ROWS, N_MOD, D_MODEL)[:, jnp.asarray(seg_row)]

    rope = _rope_tables(max(t_prompt, t_sample))
    tiles_prompt, tiles_sample = t_prompt // TOKEN_TILE, t_sample // TOKEN_TILE
    n_tiles_prompt = tok_prompt // TOKEN_TILE

    def rope_block(i):
        return jnp.where(i < n_tiles_prompt, i % tiles_prompt, (i - n_tiles_prompt) % tiles_sample)

    bias = _band_bias(rel_bias)
    w_in_b, w_out_b = w_in.astype(BF16), w_out.astype(BF16)
    w_up_b, w_down_b = w_up.astype(BF16), w_down.astype(BF16)

    for l in range(DEPTH):
        qat, ka, vat, qbt, kb, vbt = _in_projection(x_parts, mod[l], norm_attn[l], w_in_b, l,
                                                    q_norm[l], k_norm[l], rope, rope_block)
        bound = _score_bound(q_norm[l], k_norm[l])
        out_a_parts = (_global_attention(bound, qat, ka, vat, 0, t_prompt, n_prompt),
                       _global_attention(bound, qat, ka, vat, tok_prompt, t_sample, n_sample))
        x, h = _mix_tail(out_a_parts, x_parts, qbt, kb, vbt, bias, sink[l], seq_bounds, mod[l],
                         out_norm_a[l], out_norm_b[l], w_out_b, l, norm_mlp[l])
        mlp = functools.partial(_mlp, x, h, mod[l], w_up_b, w_down_b, l, norm_final)
        if l < DEPTH - 1:
            x_parts = (mlp(False),)
    y_prompt = mlp(True, 0, tok_prompt).reshape(n_prompt, t_prompt, D_MODEL)
    y_sample = mlp(True, tok_prompt, n_tok - tok_prompt).reshape(n_sample, t_sample, D_MODEL)
    return y_prompt, y_sample
```

```python
import functools
import math

import jax
import jax.numpy as jnp
from jax import lax
from jax.experimental import pallas as pl
from jax.experimental.pallas import tpu as pltpu

D_MODEL = 2048
DEPTH = 4
HEAD_DIM = 128
A_Q_HEADS = 8
A_KV_HEADS = 2
B_Q_HEADS = 8
B_KV_HEADS = 2
GROUP = A_Q_HEADS // A_KV_HEADS
A_WIDTH = A_Q_HEADS * HEAD_DIM
B_WIDTH = B_Q_HEADS * HEAD_DIM
MIX_WIDTH = A_WIDTH + B_WIDTH
KV_WIDTH = A_KV_HEADS * HEAD_DIM
IN_WIDTH = A_WIDTH + 2 * KV_WIDTH + B_WIDTH + 2 * KV_WIDTH
D_FF = 4 * D_MODEL
Q_BLOCK = 128
WINDOW = 128
BAND = Q_BLOCK + 2 * WINDOW
NUM_BUCKETS = 32
MAX_DISTANCE = 128
GRID_W = 64
ROPE_THETA = 10000.0
EPS = 1e-6
SCALE = HEAD_DIM ** -0.5
LOG2E = math.log2(math.e)

OFF_QA = 0
OFF_KA = OFF_QA + A_WIDTH
OFF_VA = OFF_KA + KV_WIDTH
OFF_QB = OFF_VA + KV_WIDTH
OFF_KB = OFF_QB + B_WIDTH
OFF_VB = OFF_KB + KV_WIDTH

MOD_ROWS = 8
N_MOD = 6
MOD_SHIFT_MIX, MOD_SCALE_MIX, MOD_GATE_MIX, MOD_SHIFT_MLP, MOD_SCALE_MLP, MOD_GATE_MLP = range(N_MOD)
TOKEN_TILE = 512
MLP_TOKEN_TILE = 1024
FF_TILE = 512
ATTN_Q_TILE = 256
ATTN_CHUNKS = 1
ATTN_SLOTS = 3
ATTN_ROUND_UNROLL = 11
SAFE_EXP2_RANGE = 60.0
SUBLANES = 8
BF16_SUBLANES = 16
VMEM_LIMIT = 60 * 1024 * 1024

F32 = jnp.float32
BF16 = jnp.bfloat16


def _params(*semantics):
    return pltpu.CompilerParams(dimension_semantics=semantics, vmem_limit_bytes=VMEM_LIMIT)


def _rms(x):
    return x * lax.rsqrt(jnp.mean(x * x, axis=-1, keepdims=True) + EPS)


def _part_starts(parts, tm):
    starts, start = [], 0
    for p in parts:
        starts.append(start)
        start += p.shape[0] // tm
    return tuple(starts)


def _part_specs(parts, tm, width):
    specs = []
    for p, start in zip(parts, _part_starts(parts, tm)):
        n = p.shape[0] // tm
        specs.append(pl.BlockSpec((tm, width),
                                  lambda i, start=start, n=n: (jnp.clip(i - start, 0, n - 1), 0)))
    return specs


def _read_parts(refs, starts):
    i = pl.program_id(0)
    x = refs[0][...]
    for ref, start in zip(refs[1:], starts[1:]):
        x = jnp.where(i >= start, ref[...], x)
    return x


def _mod_kernel(c_ref, w_ref, b_ref, o_ref):
    c = c_ref[...]
    a = (c * jax.nn.sigmoid(c)).astype(BF16)
    o_ref[...] = jnp.dot(a, w_ref[...].astype(BF16), preferred_element_type=F32) + b_ref[...]


def _modulation(c_rows, w_mod, b_mod):
    tn = D_MODEL
    return pl.pallas_call(
        _mod_kernel,
        grid=(DEPTH, N_MOD * D_MODEL // tn),
        in_specs=[
            pl.BlockSpec((MOD_ROWS, D_MODEL), lambda l, n: (0, 0)),
            pl.BlockSpec((None, D_MODEL, tn), lambda l, n: (l, 0, n)),
            pl.BlockSpec((None, 1, tn), lambda l, n: (l, 0, n)),
        ],
        out_specs=pl.BlockSpec((None, MOD_ROWS, tn), lambda l, n: (l, 0, n)),
        out_shape=jax.ShapeDtypeStruct((DEPTH, MOD_ROWS, N_MOD * D_MODEL), F32),
        compiler_params=_params("arbitrary", "arbitrary"),
        name="modulation",
    )(c_rows, w_mod, b_mod.reshape(DEPTH, 1, N_MOD * D_MODEL))


def _bias_kernel(bucket_ref, rel_bias_ref, o_ref):
    bucket = bucket_ref[...]
    in_band = bucket >= 0
    for h in range(B_Q_HEADS):
        acc = jnp.zeros((BAND, Q_BLOCK), F32)
        for b in range(NUM_BUCKETS):
            acc = jnp.where(bucket == b, rel_bias_ref[b, h], acc)
        g = h % GROUP
        o_ref[h // GROUP, :, g * Q_BLOCK:(g + 1) * Q_BLOCK] = jnp.where(in_band, acc * LOG2E, -jnp.inf)


def _t5_bucket(rel):
    half = NUM_BUCKETS // 2
    max_exact = half // 2
    n = jnp.abs(rel)
    nf = jnp.maximum(n, 1).astype(F32)
    large = max_exact + (jnp.log(nf / max_exact) / math.log(MAX_DISTANCE / max_exact)
                         * (half - max_exact)).astype(jnp.int32)
    large = jnp.minimum(large, half - 1)
    return jnp.where(rel > 0, half, 0) + jnp.where(n < max_exact, n, large)


def _band_bias(rel_bias):
    rel = jnp.arange(BAND)[:, None] - WINDOW - jnp.arange(Q_BLOCK)[None, :]
    bucket = jnp.where(jnp.abs(rel) <= WINDOW, _t5_bucket(rel), -1).astype(jnp.int32)
    return pl.pallas_call(
        _bias_kernel,
        in_specs=[
            pl.BlockSpec(memory_space=pltpu.VMEM),
            pl.BlockSpec(memory_space=pltpu.SMEM),
        ],
        out_specs=pl.BlockSpec(memory_space=pltpu.VMEM),
        out_shape=jax.ShapeDtypeStruct((B_KV_HEADS, BAND, GROUP * Q_BLOCK), F32),
        name="band_bias",
    )(bucket, rel_bias)


def _rope_tables(t_len):
    rows = t_len // GRID_W
    row_ids = jnp.repeat(jnp.arange(rows, dtype=F32), GRID_W)
    col_ids = jnp.tile(jnp.arange(GRID_W, dtype=F32), rows)
    half = HEAD_DIM // 2
    inv_freq = ROPE_THETA ** (-jnp.arange(0, half, 2, dtype=F32) / half)
    ang_r = row_ids[:, None] * inv_freq[None, :]
    ang_c = col_ids[:, None] * inv_freq[None, :]
    ang = jnp.concatenate([ang_r, ang_r, ang_c, ang_c], axis=-1)
    cos, sin = jnp.cos(ang), jnp.sin(ang)
    quarter = (jnp.arange(HEAD_DIM) // (HEAD_DIM // 4))[None, :]
    sin_up = jnp.where(quarter % 2 == 0, -sin, 0.0)
    sin_down = jnp.where(quarter % 2 == 1, sin, 0.0)
    return cos, sin_up, sin_down


def _inproj_kernel(*refs, x_starts):
    x_refs, refs = refs[:len(x_starts)], refs[len(x_starts):]
    (mod_ref, g_ref, w_ref, qn_ref, kn_ref, cos_ref, su_ref, sd_ref,
     qat_ref, ka_ref, vat_ref, qbt_ref, kb_ref, vbt_ref) = refs
    x = _read_parts(x_refs, x_starts)
    h = (_rms(x) * g_ref[...] * (1.0 + mod_ref[MOD_SCALE_MIX:MOD_SCALE_MIX + 1, :])
         + mod_ref[MOD_SHIFT_MIX:MOD_SHIFT_MIX + 1, :]).astype(BF16)

    def proj(lo, width):
        return jnp.dot(h, w_ref[:, lo:lo + width], preferred_element_type=F32)

    cos, s_up, s_down = cos_ref[...], su_ref[...], sd_ref[...]
    quarter = HEAD_DIM // 4

    def norm_rope(z, gain):
        zn = _rms(z) * gain
        return (zn * cos + pltpu.roll(zn, HEAD_DIM - quarter, 1) * s_up
                + pltpu.roll(zn, quarter, 1) * s_down)

    qa = proj(OFF_QA, A_WIDTH)
    for hd in range(A_Q_HEADS):
        sl = slice(hd * HEAD_DIM, (hd + 1) * HEAD_DIM)
        qat_ref[hd] = (norm_rope(qa[:, sl], qn_ref[...]) * (SCALE * LOG2E)).T.astype(BF16)
    ka = proj(OFF_KA, KV_WIDTH)
    for hd in range(A_KV_HEADS):
        sl = slice(hd * HEAD_DIM, (hd + 1) * HEAD_DIM)
        ka_ref[:, sl] = norm_rope(ka[:, sl], kn_ref[...]).astype(BF16)
    va = proj(OFF_VA, KV_WIDTH)
    for hd in range(A_KV_HEADS):
        sl = slice(hd * HEAD_DIM, (hd + 1) * HEAD_DIM)
        vat_ref[hd, 0] = va[:, sl].T.astype(BF16)
    qb = proj(OFF_QB, B_WIDTH)
    for hd in range(B_Q_HEADS):
        sl = slice(hd * HEAD_DIM, (hd + 1) * HEAD_DIM)
        qbt_ref[hd] = (qb[:, sl] * (SCALE * LOG2E)).T.astype(BF16)
    kb_ref[...] = proj(OFF_KB, KV_WIDTH).astype(BF16)
    vb = proj(OFF_VB, KV_WIDTH)
    for hd in range(B_KV_HEADS):
        sl = slice(hd * HEAD_DIM, (hd + 1) * HEAD_DIM)
        vbt_ref[hd] = vb[:, sl].T.astype(BF16)


def _in_projection(x_parts, mod_rows, norm_attn, w_in, layer, q_norm, k_norm, rope, rope_block):
    n_tok = sum(p.shape[0] for p in x_parts)
    tm = TOKEN_TILE
    n_tiles = n_tok // tm
    per_seg = mod_rows.shape[0]
    seg_tiles = n_tiles // per_seg
    cos, s_up, s_down = rope
    row = lambda i: (i, 0)
    const = lambda i: (0, 0)
    rope_spec = pl.BlockSpec((tm, HEAD_DIM), lambda i: (rope_block(i), 0))
    return pl.pallas_call(
        functools.partial(_inproj_kernel, x_starts=_part_starts(x_parts, tm)),
        grid=(n_tiles,),
        in_specs=[
            *_part_specs(x_parts, tm, D_MODEL),
            pl.BlockSpec((None, N_MOD, D_MODEL), lambda i: (i // seg_tiles, 0, 0)),
            pl.BlockSpec((1, D_MODEL), const),
            pl.BlockSpec((None, D_MODEL, IN_WIDTH), lambda i: (layer, 0, 0)),
            pl.BlockSpec((1, HEAD_DIM), const),
            pl.BlockSpec((1, HEAD_DIM), const),
            rope_spec, rope_spec, rope_spec,
        ],
        out_specs=[
            pl.BlockSpec((A_Q_HEADS, HEAD_DIM, tm), lambda i: (0, 0, i)),
            pl.BlockSpec((tm, KV_WIDTH), row),
            pl.BlockSpec((A_KV_HEADS, 1, HEAD_DIM, tm), lambda i: (0, i, 0, 0)),
            pl.BlockSpec((B_Q_HEADS, HEAD_DIM, tm), lambda i: (0, 0, i)),
            pl.BlockSpec((tm, KV_WIDTH), row),
            pl.BlockSpec((B_KV_HEADS, HEAD_DIM, tm), lambda i: (0, 0, i)),
        ],
        out_shape=[
            jax.ShapeDtypeStruct((A_Q_HEADS, HEAD_DIM, n_tok), BF16),
            jax.ShapeDtypeStruct((n_tok, KV_WIDTH), BF16),
            jax.ShapeDtypeStruct((A_KV_HEADS, n_tiles, HEAD_DIM, tm), BF16),
            jax.ShapeDtypeStruct((B_Q_HEADS, HEAD_DIM, n_tok), BF16),
            jax.ShapeDtypeStruct((n_tok, KV_WIDTH), BF16),
            jax.ShapeDtypeStruct((B_KV_HEADS, HEAD_DIM, n_tok), BF16),
        ],
        compiler_params=_params("arbitrary"),
        name="in_projection",
    )(*x_parts, mod_rows, norm_attn.reshape(1, D_MODEL), w_in, q_norm.reshape(1, HEAD_DIM),
      k_norm.reshape(1, HEAD_DIM), cos, s_up, s_down)


def _global_attn_kernel(bound_ref, q_ref, k_ref, vt_ref, *refs, n_key_tiles, n_cast):
    cast_in, o_ref, cast_out = refs[:n_cast], refs[n_cast], refs[n_cast + 1:2 * n_cast + 1]
    qt_ref, scratch = refs[2 * n_cast + 1], refs[2 * n_cast + 2:]
    for w_in_ref, w_out_ref in zip(cast_in, cast_out):
        w_out_ref[...] = w_in_ref[...].astype(BF16)
    tq = q_ref.shape[-1]
    tk = vt_ref.shape[-1]
    n_lanes = GROUP * tq
    rows = tk // ATTN_CHUNKS
    cols = n_lanes // ATTN_CHUNKS
    n = ATTN_SLOTS
    s_bufs, p_bufs, a_bufs, c_bufs = (scratch[i * n:(i + 1) * n] for i in range(4))
    m_ref, l_ref, acc_ref = scratch[4 * n:]
    for g in range(GROUP):
        qt_ref[:, g * tq:(g + 1) * tq] = q_ref[g]
    bounded = bound_ref[0] <= SAFE_EXP2_RANGE
    l_ref[...] = jnp.zeros(l_ref.shape, F32)
    acc_ref[...] = jnp.zeros(acc_ref.shape, F32)

    def fold(x, op):
        return functools.reduce(op, [x[r:r + SUBLANES] for r in range(0, x.shape[0], SUBLANES)])

    def rounds(first, last, one_step, per_round=n):
        n_rounds = (last - first) // per_round

        def round_(i, carry):
            for u in range(per_round):
                one_step(first + per_round * i + u, first + u)
            return carry

        if n_rounds > 1:
            lax.fori_loop(0, n_rounds, round_, 0)
        elif n_rounds == 1:
            round_(0, 0)
        for t in range(first + n_rounds * per_round, last):
            one_step(t, t)

    def bounded_step(qk, pv):
        if qk is not None:
            l_part = l_ref[...]
        for c in range(ATTN_CHUNKS):
            rs = slice(c * rows, (c + 1) * rows)
            cs = slice(c * cols, (c + 1) * cols)
            if qk is not None:
                j, slot = qk
                kc = k_ref[pl.ds(pl.multiple_of(j * tk + c * rows, rows), rows), :]
                p = jnp.exp2(jnp.dot(kc, qt_ref[...], preferred_element_type=F32))
                l_part = l_part + fold(p, jnp.add)
                p_bufs[slot][rs, :] = p.astype(BF16)
            if pv is not None:
                j, slot = pv
                acc_ref[:, cs] += jnp.dot(vt_ref[j], p_bufs[slot][:, cs],
                                          preferred_element_type=F32)
        if qk is not None:
            l_ref[...] = l_part

    @pl.when(bounded)
    def _():
        bounded_step((0, 0), None)
        rounds(0, n_key_tiles - 1,
               lambda t, ts: bounded_step((t + 1, (ts + 1) % n), (t, ts % n)),
               per_round=n * ATTN_ROUND_UNROLL)
        bounded_step(None, (n_key_tiles - 1, (n_key_tiles - 1) % n))

    def step(qk, sm, pv):
        if sm is not None:
            m_prev = m_ref[...]
            m_new = jnp.maximum(m_prev, c_bufs[sm][...])
            alpha = jnp.exp2(m_prev - m_new)
            a_bufs[sm][...] = alpha
            m_ref[...] = m_new
            l_part = alpha * l_ref[...]
        if qk is not None:
            col_max = jnp.full((SUBLANES, n_lanes), -jnp.inf, F32)
        for c in range(ATTN_CHUNKS):
            rs = slice(c * rows, (c + 1) * rows)
            cs = slice(c * cols, (c + 1) * cols)
            if qk is not None:
                j, slot = qk
                kc = k_ref[pl.ds(pl.multiple_of(j * tk + c * rows, rows), rows), :]
                sc = jnp.dot(kc, qt_ref[...], preferred_element_type=F32)
                s_bufs[slot][rs, :] = sc
                col_max = jnp.maximum(col_max, fold(sc, jnp.maximum))
            if sm is not None:
                p = jnp.exp2(s_bufs[sm][rs, :] - m_new)
                l_part = l_part + fold(p, jnp.add)
                p_bufs[sm][rs, :] = p.astype(BF16)
            if pv is not None:
                j, slot = pv
                acc_ref[:, cs] = a_bufs[slot][:, cs] * acc_ref[:, cs] + jnp.dot(
                    vt_ref[j], p_bufs[slot][:, cs], preferred_element_type=F32)
        if qk is not None:
            c_bufs[qk[1]][...] = jnp.max(col_max, axis=0, keepdims=True)
        if sm is not None:
            l_ref[...] = l_part

    @pl.when(jnp.logical_not(bounded))
    def _():
        m_ref[...] = jnp.full(m_ref.shape, -jnp.inf, F32)
        step((0, 0), None, None)
        step((1, 1 % n), 0, None)
        rounds(1, n_key_tiles - 1,
               lambda t, ts: step((t + 1, (ts + 1) % n), ts % n, (t - 1, (ts - 1) % n)))
        step(None, (n_key_tiles - 1) % n, (n_key_tiles - 2, (n_key_tiles - 2) % n))
        step(None, None, (n_key_tiles - 1, (n_key_tiles - 1) % n))

    o = acc_ref[...] / jnp.sum(l_ref[...], axis=0, keepdims=True)
    for g in range(GROUP):
        o_ref[:, g * HEAD_DIM:(g + 1) * HEAD_DIM] = o[:, g * tq:(g + 1) * tq].T.astype(BF16)


def _score_bound(q_norm, k_norm):
    return (HEAD_DIM * SCALE * LOG2E * jnp.max(jnp.abs(q_norm)) * jnp.max(jnp.abs(k_norm))
            ).astype(F32).reshape(1)


def _global_attention(score_bound, qat, ka, vat, tok0, seq_len, n_seq, cast=()):
    tq, tk = ATTN_Q_TILE, TOKEN_TILE
    q_tiles = seq_len // tq
    k_tiles = seq_len // tk
    assert tok0 % seq_len == 0 and k_tiles >= 2
    q0 = tok0 // tq
    s0 = tok0 // seq_len
    n_lanes = GROUP * tq
    n_steps = n_seq * A_KV_HEADS * q_tiles
    step = lambda b, h, i: (b * A_KV_HEADS + h) * q_tiles + i
    cast_in_specs, cast_out_specs, cast_shapes = [], [], []
    for w, layer in cast:
        _, rows, cols = w.shape
        slab = rows // n_steps
        assert slab * n_steps == rows and slab % BF16_SUBLANES == 0
        cast_in_specs.append(pl.BlockSpec(
            (None, slab, cols), lambda b, h, i, layer=layer: (layer, step(b, h, i), 0)))
        cast_out_specs.append(pl.BlockSpec((None, slab, cols), lambda b, h, i: (0, step(b, h, i), 0)))
        cast_shapes.append(jax.ShapeDtypeStruct((1, rows, cols), BF16))
    out, *casts = pl.pallas_call(
        functools.partial(_global_attn_kernel, n_key_tiles=k_tiles, n_cast=len(cast)),
        grid=(n_seq, A_KV_HEADS, q_tiles),
        in_specs=[
            pl.BlockSpec(memory_space=pltpu.SMEM),
            pl.BlockSpec((GROUP, HEAD_DIM, tq), lambda b, h, i: (h, 0, q0 + b * q_tiles + i)),
            pl.BlockSpec((seq_len, HEAD_DIM), lambda b, h, i: (s0 + b, h)),
            pl.BlockSpec((None, k_tiles, HEAD_DIM, tk), lambda b, h, i: (h, s0 + b, 0, 0)),
            *cast_in_specs,
        ],
        out_specs=[pl.BlockSpec((tq, GROUP * HEAD_DIM), lambda b, h, i: (b * q_tiles + i, h)),
                   *cast_out_specs],
        out_shape=[jax.ShapeDtypeStruct((n_seq * seq_len, A_WIDTH), BF16), *cast_shapes],
        scratch_shapes=[
            pltpu.VMEM((HEAD_DIM, n_lanes), BF16),
            *[pltpu.VMEM((tk, n_lanes), F32)] * ATTN_SLOTS,
            *[pltpu.VMEM((tk, n_lanes), BF16)] * ATTN_SLOTS,
            *[pltpu.VMEM((1, n_lanes), F32)] * ATTN_SLOTS,
            *[pltpu.VMEM((1, n_lanes), F32)] * ATTN_SLOTS,
            pltpu.VMEM((1, n_lanes), F32),
            pltpu.VMEM((SUBLANES, n_lanes), F32),
            pltpu.VMEM((HEAD_DIM, n_lanes), F32),
        ],
        compiler_params=_params("arbitrary", "arbitrary", "arbitrary"),
        name="global_attention",
    )(score_bound, qat, ka, vat, *[w for w, _ in cast])
    return out, casts


def _mix_kernel(*refs, a_starts, x_starts, seq_starts, seq_ends):
    a_refs, refs = refs[:len(a_starts)], refs[len(a_starts):]
    x_refs, refs = refs[:len(x_starts)], refs[len(x_starts):]
    (sink_ref, qt_ref, kp_ref, kc_ref, kn_ref, vtp_ref, vtc_ref, vtn_ref, bias_ref,
     mod_ref, ga_ref, gb_ref, w_ref, gm_ref, o_ref, h_ref, ob_ref) = refs
    i = pl.program_id(0)

    mix_a = (_rms(_read_parts(a_refs, a_starts).astype(F32)) * ga_ref[...]).astype(BF16)
    x_in = _read_parts(x_refs, x_starts)
    gate = mod_ref[MOD_GATE_MIX:MOD_GATE_MIX + 1, :]
    n_chains = B_KV_HEADS * (qt_ref.shape[-1] // Q_BLOCK)
    a_cols = D_MODEL // n_chains

    def project_a(c):
        cs = slice(c * a_cols, (c + 1) * a_cols)
        o_ref[:, cs] = x_in[:, cs] + gate[:, cs] * jnp.dot(
            mix_a, w_ref[:A_WIDTH, cs], preferred_element_type=F32)

    sub_blocks = qt_ref.shape[-1] // Q_BLOCK
    n_lanes = GROUP * Q_BLOCK
    lane_group = lax.broadcasted_iota(jnp.int32, (1, n_lanes), 1) // Q_BLOCK
    neg_inf = jnp.full((WINDOW, n_lanes), -jnp.inf, F32)

    def fold(x, op):
        return functools.reduce(op, [x[r:r + SUBLANES] for r in range(0, x.shape[0], SUBLANES)])

    for h in range(B_KV_HEADS):
        hs = slice(h * HEAD_DIM, (h + 1) * HEAD_DIM)
        k_all = jnp.concatenate([kp_ref[:, hs], kc_ref[:, hs], kn_ref[:, hs]], axis=0)
        vt_all = jnp.concatenate([vtp_ref[h], vtc_ref[h], vtn_ref[h]], axis=1)
        sink = jnp.full((1, n_lanes), sink_ref[h * GROUP] * LOG2E, F32)
        for g in range(1, GROUP):
            sink = jnp.where(lane_group == g, sink_ref[h * GROUP + g] * LOG2E, sink)
        for u in range(sub_blocks):
            project_a(h * sub_blocks + u)
            blk = i * sub_blocks + u
            first = functools.reduce(jnp.logical_or, [blk == s for s in seq_starts])
            last = functools.reduce(jnp.logical_or, [blk == e - 1 for e in seq_ends])
            qs = slice(u * Q_BLOCK, (u + 1) * Q_BLOCK)
            qt = jnp.concatenate([qt_ref[h * GROUP + g, :, qs] for g in range(GROUP)], axis=1)
            kw = k_all[u * Q_BLOCK:u * Q_BLOCK + BAND]
            s = jnp.dot(kw, qt, preferred_element_type=F32) + bias_ref[h]
            s_prev = jnp.where(first, neg_inf, s[:WINDOW])
            s_mid = s[WINDOW:WINDOW + Q_BLOCK]
            s_next = jnp.where(last, neg_inf, s[WINDOW + Q_BLOCK:])
            col_max = functools.reduce(
                jnp.maximum, [fold(v, jnp.maximum) for v in (s_prev, s_mid, s_next)])
            m = jnp.maximum(jnp.max(col_max, axis=0, keepdims=True), sink)
            e = [jnp.exp2(v - m) for v in (s_prev, s_mid, s_next)]
            col_sum = functools.reduce(jnp.add, [fold(v, jnp.add) for v in e])
            den = jnp.sum(col_sum, axis=0, keepdims=True) + jnp.exp2(sink - m)
            p = jnp.concatenate([v.astype(BF16) for v in e], axis=0)
            ot = jnp.dot(vt_all[:, u * Q_BLOCK:u * Q_BLOCK + BAND], p,
                         preferred_element_type=F32) / den
            for g in range(GROUP):
                ob_ref[qs, (h * GROUP + g) * HEAD_DIM:(h * GROUP + g + 1) * HEAD_DIM] = (
                    ot[:, g * Q_BLOCK:(g + 1) * Q_BLOCK].T.astype(BF16))

    mix_b = (_rms(ob_ref[...].astype(F32)) * gb_ref[...]).astype(BF16)
    x = o_ref[...] + gate * jnp.dot(mix_b, w_ref[A_WIDTH:, :], preferred_element_type=F32)
    o_ref[...] = x
    h_ref[...] = (_rms(x) * gm_ref[...] * (1.0 + mod_ref[MOD_SCALE_MLP:MOD_SCALE_MLP + 1, :])
                  + mod_ref[MOD_SHIFT_MLP:MOD_SHIFT_MLP + 1, :]).astype(BF16)


def _mix_tail(out_a_parts, x_parts, qbt, kb, vbt, bias, sink, seq_bounds, mod_rows, out_norm_a,
              out_norm_b, w_out, layer, norm_mlp):
    n_tok = kb.shape[0]
    tm = TOKEN_TILE
    n_tiles = n_tok // tm
    sub = tm // Q_BLOCK
    n_blocks = n_tok // Q_BLOCK
    seg_tiles = n_tiles // mod_rows.shape[0]
    seq_starts = tuple(s // Q_BLOCK for s, _ in seq_bounds)
    seq_ends = tuple(e // Q_BLOCK for _, e in seq_bounds)
    row = lambda i: (i, 0)
    const = lambda i: (0, 0)
    prev_block = lambda i: jnp.maximum(i * sub - 1, 0)
    next_block = lambda i: jnp.minimum(i * sub + sub, n_blocks - 1)
    k_prev = pl.BlockSpec((Q_BLOCK, KV_WIDTH), lambda i: (prev_block(i), 0))
    k_next = pl.BlockSpec((Q_BLOCK, KV_WIDTH), lambda i: (next_block(i), 0))
    vt_cur = pl.BlockSpec((B_KV_HEADS, HEAD_DIM, tm), lambda i: (0, 0, i))
    vt_prev = pl.BlockSpec((B_KV_HEADS, HEAD_DIM, Q_BLOCK), lambda i: (0, 0, prev_block(i)))
    vt_next = pl.BlockSpec((B_KV_HEADS, HEAD_DIM, Q_BLOCK), lambda i: (0, 0, next_block(i)))
    return pl.pallas_call(
        functools.partial(_mix_kernel, a_starts=_part_starts(out_a_parts, tm),
                          x_starts=_part_starts(x_parts, tm), seq_starts=seq_starts,
                          seq_ends=seq_ends),
        grid=(n_tiles,),
        in_specs=[
            *_part_specs(out_a_parts, tm, A_WIDTH),
            *_part_specs(x_parts, tm, D_MODEL),
            pl.BlockSpec(memory_space=pltpu.SMEM),
            pl.BlockSpec((B_Q_HEADS, HEAD_DIM, tm), lambda i: (0, 0, i)),
            k_prev, pl.BlockSpec((tm, KV_WIDTH), row), k_next, vt_prev, vt_cur, vt_next,
            pl.BlockSpec((B_KV_HEADS, BAND, GROUP * Q_BLOCK), lambda i: (0, 0, 0)),
            pl.BlockSpec((None, N_MOD, D_MODEL), lambda i: (i // seg_tiles, 0, 0)),
            pl.BlockSpec((1, A_WIDTH), const),
            pl.BlockSpec((1, B_WIDTH), const),
            pl.BlockSpec((None, MIX_WIDTH, D_MODEL), lambda i: (layer, 0, 0)),
            pl.BlockSpec((1, D_MODEL), const),
        ],
        out_specs=[pl.BlockSpec((tm, D_MODEL), row), pl.BlockSpec((tm, D_MODEL), row)],
        out_shape=[jax.ShapeDtypeStruct((n_tok, D_MODEL), F32),
                   jax.ShapeDtypeStruct((n_tok, D_MODEL), BF16)],
        scratch_shapes=[pltpu.VMEM((tm, B_WIDTH), BF16)],
        compiler_params=_params("arbitrary"),
        name="mix_tail",
    )(*out_a_parts, *x_parts, sink, qbt, kb, kb, kb, vbt, vbt, vbt, bias, mod_rows,
      out_norm_a.reshape(1, A_WIDTH), out_norm_b.reshape(1, B_WIDTH), w_out,
      norm_mlp.reshape(1, D_MODEL))


def _mlp_kernel(x_ref, h_ref, mod_ref, wu_ref, wd_ref, gf_ref, o_ref, *, final_norm):
    k = pl.program_id(1)

    @pl.when(k == 0)
    def _():
        o_ref[...] = jnp.zeros(o_ref.shape, F32)

    u = jnp.maximum(jnp.dot(h_ref[...], wu_ref[...], preferred_element_type=F32), 0.0)
    o_ref[...] += jnp.dot((u * u).astype(BF16), wd_ref[...], preferred_element_type=F32)

    @pl.when(k == pl.num_programs(1) - 1)
    def _():
        y = x_ref[...] + mod_ref[MOD_GATE_MLP:MOD_GATE_MLP + 1, :] * o_ref[...]
        if final_norm:
            y = _rms(y) * gf_ref[...]
        o_ref[...] = y


def _mlp(x, h, mod_rows, w_up, w_down, layer, norm_final, final_norm, tok0=0, n_out=None):
    n_tok = x.shape[0]
    n_out = n_tok if n_out is None else n_out
    tm, tf = MLP_TOKEN_TILE, FF_TILE
    seg = n_tok // mod_rows.shape[0]
    tm = min(tm, seg)
    assert tok0 % tm == 0 and n_out % tm == 0
    t0, seg_tiles = tok0 // tm, seg // tm
    return pl.pallas_call(
        functools.partial(_mlp_kernel, final_norm=final_norm),
        grid=(n_out // tm, D_FF // tf),
        in_specs=[
            pl.BlockSpec((tm, D_MODEL), lambda i, k: (t0 + i, 0)),
            pl.BlockSpec((tm, D_MODEL), lambda i, k: (t0 + i, 0)),
            pl.BlockSpec((None, N_MOD, D_MODEL), lambda i, k: ((t0 + i) // seg_tiles, 0, 0)),
            pl.BlockSpec((None, D_MODEL, tf), lambda i, k: (layer, 0, k)),
            pl.BlockSpec((None, tf, D_MODEL), lambda i, k: (layer, k, 0)),
            pl.BlockSpec((1, D_MODEL), lambda i, k: (0, 0)),
        ],
        out_specs=pl.BlockSpec((tm, D_MODEL), lambda i, k: (i, 0)),
        out_shape=jax.ShapeDtypeStruct((n_out, D_MODEL), F32),
        compiler_params=_params("arbitrary", "arbitrary"),
        name="mlp",
    )(x, h, mod_rows, w_up, w_down, norm_final.reshape(1, D_MODEL))


def kernel(x_prompt, x_sample, c_prompt, c_sample, w_mod, b_mod, norm_attn, w_in, q_norm, k_norm,
           sink, out_norm_a, out_norm_b, w_out, norm_mlp, w_up, w_down, rel_bias, norm_final):
    n_prompt, t_prompt, _ = x_prompt.shape
    n_sample, t_sample, _ = x_sample.shape
    tok_prompt = n_prompt * t_prompt
    n_tok = tok_prompt + n_sample * t_sample
    seg = math.gcd(t_prompt, t_sample)
    seg_row = ([b for b in range(n_prompt) for _ in range(t_prompt // seg)]
               + [n_prompt + b for b in range(n_sample) for _ in range(t_sample // seg)])
    seq_bounds = ([(b * t_prompt, (b + 1) * t_prompt) for b in range(n_prompt)]
                  + [(tok_prompt + b * t_sample, tok_prompt + (b + 1) * t_sample)
                     for b in range(n_sample)])

    x_parts = (x_prompt.reshape(tok_prompt, D_MODEL), x_sample.reshape(n_tok - tok_prompt, D_MODEL))
    c_rows = jnp.concatenate([c_prompt, c_sample], axis=0)
    c_rows = jnp.pad(c_rows, ((0, MOD_ROWS - c_rows.shape[0]), (0, 0)))
    mod = _modulation(c_rows, w_mod, b_mod)
    mod = mod.reshape(DEPTH, MOD_ROWS, N_MOD, D_MODEL)[:, jnp.asarray(seg_row)]

    rope = _rope_tables(max(t_prompt, t_sample))
    tiles_prompt, tiles_sample = t_prompt // TOKEN_TILE, t_sample // TOKEN_TILE
    n_tiles_prompt = tok_prompt // TOKEN_TILE

    def rope_block(i):
        return jnp.where(i < n_tiles_prompt, i % tiles_prompt, (i - n_tiles_prompt) % tiles_sample)

    bias = _band_bias(rel_bias)
    w_in_b = w_in[:1].astype(BF16)

    for l in range(DEPTH):
        qat, ka, vat, qbt, kb, vbt = _in_projection(x_parts, mod[l], norm_attn[l], w_in_b, 0,
                                                    q_norm[l], k_norm[l], rope, rope_block)
        bound = _score_bound(q_norm[l], k_norm[l])
        cast = [(w_out, l), (w_up, l), (w_down, l)] + ([(w_in, l + 1)] if l + 1 < DEPTH else [])
        out_a_prompt, (w_out_b, w_up_b, w_down_b, *w_in_next) = _global_attention(
            bound, qat, ka, vat, 0, t_prompt, n_prompt, cast)
        out_a_sample, _ = _global_attention(bound, qat, ka, vat, tok_prompt, t_sample, n_sample)
        out_a_parts = (out_a_prompt, out_a_sample)
        x, h = _mix_tail(out_a_parts, x_parts, qbt, kb, vbt, bias, sink[l], seq_bounds, mod[l],
                         out_norm_a[l], out_norm_b[l], w_out_b, 0, norm_mlp[l])
        mlp = functools.partial(_mlp, x, h, mod[l], w_up_b, w_down_b, 0, norm_final)
        if w_in_next:
            w_in_b = w_in_next[0]
        if l < DEPTH - 1:
            x_parts = (mlp(False),)
    y_prompt = mlp(True, 0, tok_prompt).reshape(n_prompt, t_prompt, D_MODEL)
    y_sample = mlp(True, tok_prompt, n_tok - tok_prompt).reshape(n_sample, t_sample, D_MODEL)
    return y_prompt, y_sample
```

```python
import functools
import math

import jax
import jax.numpy as jnp
from jax import lax
from jax.experimental import pallas as pl
from jax.experimental.pallas import tpu as pltpu

D_MODEL = 2048
DEPTH = 4
HEAD_DIM = 128
A_Q_HEADS = 8
A_KV_HEADS = 2
B_Q_HEADS = 8
B_KV_HEADS = 2
GROUP = A_Q_HEADS // A_KV_HEADS
A_WIDTH = A_Q_HEADS * HEAD_DIM
B_WIDTH = B_Q_HEADS * HEAD_DIM
MIX_WIDTH = A_WIDTH + B_WIDTH
KV_WIDTH = A_KV_HEADS * HEAD_DIM
IN_WIDTH = A_WIDTH + 2 * KV_WIDTH + B_WIDTH + 2 * KV_WIDTH
D_FF = 4 * D_MODEL
Q_BLOCK = 128
WINDOW = 128
BAND = Q_BLOCK + 2 * WINDOW
NUM_BUCKETS = 32
MAX_DISTANCE = 128
GRID_W = 64
ROPE_THETA = 10000.0
EPS = 1e-6
SCALE = HEAD_DIM ** -0.5
LOG2E = math.log2(math.e)

OFF_QA = 0
OFF_KA = OFF_QA + A_WIDTH
OFF_VA = OFF_KA + KV_WIDTH
OFF_QB = OFF_VA + KV_WIDTH
OFF_KB = OFF_QB + B_WIDTH
OFF_VB = OFF_KB + KV_WIDTH

MOD_ROWS = 8
N_MOD = 6
MOD_SHIFT_MIX, MOD_SCALE_MIX, MOD_GATE_MIX, MOD_SHIFT_MLP, MOD_SCALE_MLP, MOD_GATE_MLP = range(N_MOD)
TOKEN_TILE = 512
MLP_TOKEN_TILE = 1024
FF_TILE = 512
ATTN_Q_TILE = 256
ATTN_CHUNKS = 1
ATTN_SLOTS = 3
ATTN_ROUND_UNROLL = 11
SAFE_EXP2_RANGE = 60.0
SUBLANES = 8
BF16_SUBLANES = 16
VMEM_LIMIT = 60 * 1024 * 1024

F32 = jnp.float32
BF16 = jnp.bfloat16


def _params(*semantics):
    return pltpu.CompilerParams(dimension_semantics=semantics, vmem_limit_bytes=VMEM_LIMIT)


def _rms(x):
    return x * lax.rsqrt(jnp.mean(x * x, axis=-1, keepdims=True) + EPS)


def _part_starts(parts, tm):
    starts, start = [], 0
    for p in parts:
        starts.append(start)
        start += p.shape[0] // tm
    return tuple(starts)


def _part_specs(parts, tm, width):
    specs = []
    for p, start in zip(parts, _part_starts(parts, tm)):
        n = p.shape[0] // tm
        specs.append(pl.BlockSpec((tm, width),
                                  lambda i, start=start, n=n: (jnp.clip(i - start, 0, n - 1), 0)))
    return specs


def _read_parts(refs, starts):
    i = pl.program_id(0)
    x = refs[0][...]
    for ref, start in zip(refs[1:], starts[1:]):
        x = jnp.where(i >= start, ref[...], x)
    return x


def _mod_kernel(c_ref, w_ref, b_ref, o_ref):
    c = c_ref[...]
    a = (c * jax.nn.sigmoid(c)).astype(BF16)
    o_ref[...] = jnp.dot(a, w_ref[...].astype(BF16), preferred_element_type=F32) + b_ref[...]


def _modulation(c_rows, w_mod, b_mod):
    tn = D_MODEL
    return pl.pallas_call(
        _mod_kernel,
        grid=(DEPTH, N_MOD * D_MODEL // tn),
        in_specs=[
            pl.BlockSpec((MOD_ROWS, D_MODEL), lambda l, n: (0, 0)),
            pl.BlockSpec((None, D_MODEL, tn), lambda l, n: (l, 0, n)),
            pl.BlockSpec((None, 1, tn), lambda l, n: (l, 0, n)),
        ],
        out_specs=pl.BlockSpec((None, MOD_ROWS, tn), lambda l, n: (l, 0, n)),
        out_shape=jax.ShapeDtypeStruct((DEPTH, MOD_ROWS, N_MOD * D_MODEL), F32),
        compiler_params=_params("arbitrary", "arbitrary"),
        name="modulation",
    )(c_rows, w_mod, b_mod.reshape(DEPTH, 1, N_MOD * D_MODEL))


def _bias_kernel(bucket_ref, rel_bias_ref, o_ref):
    bucket = bucket_ref[...]
    in_band = bucket >= 0
    for h in range(B_Q_HEADS):
        acc = jnp.zeros((BAND, Q_BLOCK), F32)
        for b in range(NUM_BUCKETS):
            acc = jnp.where(bucket == b, rel_bias_ref[b, h], acc)
        g = h % GROUP
        o_ref[h // GROUP, :, g * Q_BLOCK:(g + 1) * Q_BLOCK] = jnp.where(in_band, acc * LOG2E, -jnp.inf)


def _t5_bucket(rel):
    half = NUM_BUCKETS // 2
    max_exact = half // 2
    n = jnp.abs(rel)
    nf = jnp.maximum(n, 1).astype(F32)
    large = max_exact + (jnp.log(nf / max_exact) / math.log(MAX_DISTANCE / max_exact)
                         * (half - max_exact)).astype(jnp.int32)
    large = jnp.minimum(large, half - 1)
    return jnp.where(rel > 0, half, 0) + jnp.where(n < max_exact, n, large)


def _band_bias(rel_bias):
    rel = jnp.arange(BAND)[:, None] - WINDOW - jnp.arange(Q_BLOCK)[None, :]
    bucket = jnp.where(jnp.abs(rel) <= WINDOW, _t5_bucket(rel), -1).astype(jnp.int32)
    return pl.pallas_call(
        _bias_kernel,
        in_specs=[
            pl.BlockSpec(memory_space=pltpu.VMEM),
            pl.BlockSpec(memory_space=pltpu.SMEM),
        ],
        out_specs=pl.BlockSpec(memory_space=pltpu.VMEM),
        out_shape=jax.ShapeDtypeStruct((B_KV_HEADS, BAND, GROUP * Q_BLOCK), F32),
        name="band_bias",
    )(bucket, rel_bias)


def _rope_tables(t_len):
    rows = t_len // GRID_W
    row_ids = jnp.repeat(jnp.arange(rows, dtype=F32), GRID_W)
    col_ids = jnp.tile(jnp.arange(GRID_W, dtype=F32), rows)
    half = HEAD_DIM // 2
    inv_freq = ROPE_THETA ** (-jnp.arange(0, half, 2, dtype=F32) / half)
    ang_r = row_ids[:, None] * inv_freq[None, :]
    ang_c = col_ids[:, None] * inv_freq[None, :]
    ang = jnp.concatenate([ang_r, ang_r, ang_c, ang_c], axis=-1)
    cos, sin = jnp.cos(ang), jnp.sin(ang)
    quarter = (jnp.arange(HEAD_DIM) // (HEAD_DIM // 4))[None, :]
    sin_up = jnp.where(quarter % 2 == 0, -sin, 0.0)
    sin_down = jnp.where(quarter % 2 == 1, sin, 0.0)
    return cos, sin_up, sin_down


def _inproj_kernel(*refs, x_starts):
    x_refs, refs = refs[:len(x_starts)], refs[len(x_starts):]
    (mod_ref, g_ref, w_ref, qn_ref, kn_ref, cos_ref, su_ref, sd_ref,
     qat_ref, ka_ref, vat_ref, qbt_ref, kb_ref, vbt_ref) = refs
    x = _read_parts(x_refs, x_starts)
    h = (_rms(x) * g_ref[...] * (1.0 + mod_ref[MOD_SCALE_MIX:MOD_SCALE_MIX + 1, :])
         + mod_ref[MOD_SHIFT_MIX:MOD_SHIFT_MIX + 1, :]).astype(BF16)

    def proj(lo, width):
        return jnp.dot(h, w_ref[:, lo:lo + width], preferred_element_type=F32)

    cos, s_up, s_down = cos_ref[...], su_ref[...], sd_ref[...]
    quarter = HEAD_DIM // 4

    def norm_rope(z, gain):
        zn = _rms(z) * gain
        return (zn * cos + pltpu.roll(zn, HEAD_DIM - quarter, 1) * s_up
                + pltpu.roll(zn, quarter, 1) * s_down)

    qa = proj(OFF_QA, A_WIDTH)
    for hd in range(A_Q_HEADS):
        sl = slice(hd * HEAD_DIM, (hd + 1) * HEAD_DIM)
        qat_ref[hd] = (norm_rope(qa[:, sl], qn_ref[...]) * (SCALE * LOG2E)).T.astype(BF16)
    ka = proj(OFF_KA, KV_WIDTH)
    for hd in range(A_KV_HEADS):
        sl = slice(hd * HEAD_DIM, (hd + 1) * HEAD_DIM)
        ka_ref[:, sl] = norm_rope(ka[:, sl], kn_ref[...]).astype(BF16)
    va = proj(OFF_VA, KV_WIDTH)
    for hd in range(A_KV_HEADS):
        sl = slice(hd * HEAD_DIM, (hd + 1) * HEAD_DIM)
        vat_ref[hd, 0] = va[:, sl].T.astype(BF16)
    qb = proj(OFF_QB, B_WIDTH)
    for hd in range(B_Q_HEADS):
        sl = slice(hd * HEAD_DIM, (hd + 1) * HEAD_DIM)
        qbt_ref[hd] = (qb[:, sl] * (SCALE * LOG2E)).T.astype(BF16)
    kb_ref[...] = proj(OFF_KB, KV_WIDTH).astype(BF16)
    vb = proj(OFF_VB, KV_WIDTH)
    for hd in range(B_KV_HEADS):
        sl = slice(hd * HEAD_DIM, (hd + 1) * HEAD_DIM)
        vbt_ref[hd] = vb[:, sl].T.astype(BF16)


def _in_projection(x_parts, mod_rows, norm_attn, w_in, layer, q_norm, k_norm, rope, rope_block):
    n_tok = sum(p.shape[0] for p in x_parts)
    tm = TOKEN_TILE
    n_tiles = n_tok // tm
    per_seg = mod_rows.shape[0]
    seg_tiles = n_tiles // per_seg
    cos, s_up, s_down = rope
    row = lambda i: (i, 0)
    const = lambda i: (0, 0)
    rope_spec = pl.BlockSpec((tm, HEAD_DIM), lambda i: (rope_block(i), 0))
    return pl.pallas_call(
        functools.partial(_inproj_kernel, x_starts=_part_starts(x_parts, tm)),
        grid=(n_tiles,),
        in_specs=[
            *_part_specs(x_parts, tm, D_MODEL),
            pl.BlockSpec((None, N_MOD, D_MODEL), lambda i: (i // seg_tiles, 0, 0)),
            pl.BlockSpec((1, D_MODEL), const),
            pl.BlockSpec((None, D_MODEL, IN_WIDTH), lambda i: (layer, 0, 0)),
            pl.BlockSpec((1, HEAD_DIM), const),
            pl.BlockSpec((1, HEAD_DIM), const),
            rope_spec, rope_spec, rope_spec,
        ],
        out_specs=[
            pl.BlockSpec((A_Q_HEADS, HEAD_DIM, tm), lambda i: (0, 0, i)),
            pl.BlockSpec((tm, KV_WIDTH), row),
            pl.BlockSpec((A_KV_HEADS, 1, HEAD_DIM, tm), lambda i: (0, i, 0, 0)),
            pl.BlockSpec((B_Q_HEADS, HEAD_DIM, tm), lambda i: (0, 0, i)),
            pl.BlockSpec((tm, KV_WIDTH), row),
            pl.BlockSpec((B_KV_HEADS, HEAD_DIM, tm), lambda i: (0, 0, i)),
        ],
        out_shape=[
            jax.ShapeDtypeStruct((A_Q_HEADS, HEAD_DIM, n_tok), BF16),
            jax.ShapeDtypeStruct((n_tok, KV_WIDTH), BF16),
            jax.ShapeDtypeStruct((A_KV_HEADS, n_tiles, HEAD_DIM, tm), BF16),
            jax.ShapeDtypeStruct((B_Q_HEADS, HEAD_DIM, n_tok), BF16),
            jax.ShapeDtypeStruct((n_tok, KV_WIDTH), BF16),
            jax.ShapeDtypeStruct((B_KV_HEADS, HEAD_DIM, n_tok), BF16),
        ],
        compiler_params=_params("arbitrary"),
        name="in_projection",
    )(*x_parts, mod_rows, norm_attn.reshape(1, D_MODEL), w_in, q_norm.reshape(1, HEAD_DIM),
      k_norm.reshape(1, HEAD_DIM), cos, s_up, s_down)


def _global_attn_kernel(bound_ref, q_ref, k_ref, vt_ref, *refs, n_key_tiles, n_cast):
    cast_in, o_ref, cast_out = refs[:n_cast], refs[n_cast], refs[n_cast + 1:2 * n_cast + 1]
    qt_ref, scratch = refs[2 * n_cast + 1], refs[2 * n_cast + 2:]
    for w_in_ref, w_out_ref in zip(cast_in, cast_out):
        w_out_ref[...] = w_in_ref[...].astype(BF16)
    tq = q_ref.shape[-1]
    tk = vt_ref.shape[-1]
    n_lanes = GROUP * tq
    rows = tk // ATTN_CHUNKS
    cols = n_lanes // ATTN_CHUNKS
    n = ATTN_SLOTS
    s_bufs, p_bufs, a_bufs, c_bufs = (scratch[i * n:(i + 1) * n] for i in range(4))
    m_ref, l_ref, acc_ref = scratch[4 * n:]
    for g in range(GROUP):
        qt_ref[:, g * tq:(g + 1) * tq] = q_ref[g]
    bounded = bound_ref[0] <= SAFE_EXP2_RANGE
    l_ref[...] = jnp.zeros(l_ref.shape, F32)
    acc_ref[...] = jnp.zeros(acc_ref.shape, F32)

    def fold(x, op):
        return functools.reduce(op, [x[r:r + SUBLANES] for r in range(0, x.shape[0], SUBLANES)])

    def rounds(first, last, one_step, per_round=n):
        n_rounds = (last - first) // per_round

        def round_(i, carry):
            for u in range(per_round):
                one_step(first + per_round * i + u, first + u)
            return carry

        if n_rounds > 1:
            lax.fori_loop(0, n_rounds, round_, 0)
        elif n_rounds == 1:
            round_(0, 0)
        for t in range(first + n_rounds * per_round, last):
            one_step(t, t)

    def bounded_step(qk, pv):
        if qk is not None:
            l_part = l_ref[...]
        for c in range(ATTN_CHUNKS):
            rs = slice(c * rows, (c + 1) * rows)
            cs = slice(c * cols, (c + 1) * cols)
            if qk is not None:
                j, slot = qk
                kc = k_ref[pl.ds(pl.multiple_of(j * tk + c * rows, rows), rows), :]
                p = jnp.exp2(jnp.dot(kc, qt_ref[...], preferred_element_type=F32))
                l_part = l_part + fold(p, jnp.add)
                p_bufs[slot][rs, :] = p.astype(BF16)
            if pv is not None:
                j, slot = pv
                acc_ref[:, cs] += jnp.dot(vt_ref[j], p_bufs[slot][:, cs],
                                          preferred_element_type=F32)
        if qk is not None:
            l_ref[...] = l_part

    @pl.when(bounded)
    def _():
        bounded_step((0, 0), None)
        rounds(0, n_key_tiles - 1,
               lambda t, ts: bounded_step((t + 1, (ts + 1) % n), (t, ts % n)),
               per_round=n * ATTN_ROUND_UNROLL)
        bounded_step(None, (n_key_tiles - 1, (n_key_tiles - 1) % n))

    def step(qk, sm, pv):
        if sm is not None:
            m_prev = m_ref[...]
            m_new = jnp.maximum(m_prev, c_bufs[sm][...])
            alpha = jnp.exp2(m_prev - m_new)
            a_bufs[sm][...] = alpha
            m_ref[...] = m_new
            l_part = alpha * l_ref[...]
        if qk is not None:
            col_max = jnp.full((SUBLANES, n_lanes), -jnp.inf, F32)
        for c in range(ATTN_CHUNKS):
            rs = slice(c * rows, (c + 1) * rows)
            cs = slice(c * cols, (c + 1) * cols)
            if qk is not None:
                j, slot = qk
                kc = k_ref[pl.ds(pl.multiple_of(j * tk + c * rows, rows), rows), :]
                sc = jnp.dot(kc, qt_ref[...], preferred_element_type=F32)
                s_bufs[slot][rs, :] = sc
                col_max = jnp.maximum(col_max, fold(sc, jnp.maximum))
            if sm is not None:
                p = jnp.exp2(s_bufs[sm][rs, :] - m_new)
                l_part = l_part + fold(p, jnp.add)
                p_bufs[sm][rs, :] = p.astype(BF16)
            if pv is not None:
                j, slot = pv
                acc_ref[:, cs] = a_bufs[slot][:, cs] * acc_ref[:, cs] + jnp.dot(
                    vt_ref[j], p_bufs[slot][:, cs], preferred_element_type=F32)
        if qk is not None:
            c_bufs[qk[1]][...] = jnp.max(col_max, axis=0, keepdims=True)
        if sm is not None:
            l_ref[...] = l_part

    @pl.when(jnp.logical_not(bounded))
    def _():
        m_ref[...] = jnp.full(m_ref.shape, -jnp.inf, F32)
        step((0, 0), None, None)
        step((1, 1 % n), 0, None)
        rounds(1, n_key_tiles - 1,
               lambda t, ts: step((t + 1, (ts + 1) % n), ts % n, (t - 1, (ts - 1) % n)))
        step(None, (n_key_tiles - 1) % n, (n_key_tiles - 2, (n_key_tiles - 2) % n))
        step(None, None, (n_key_tiles - 1, (n_key_tiles - 1) % n))

    o = acc_ref[...] / jnp.sum(l_ref[...], axis=0, keepdims=True)
    for g in range(GROUP):
        o_ref[:, g * HEAD_DIM:(g + 1) * HEAD_DIM] = o[:, g * tq:(g + 1) * tq].T.astype(BF16)


def _score_bound(q_norm, k_norm):
    return (HEAD_DIM * SCALE * LOG2E * jnp.max(jnp.abs(q_norm)) * jnp.max(jnp.abs(k_norm))
            ).astype(F32).reshape(1)


def _global_attention(score_bound, qat, ka, vat, tok0, seq_len, n_seq, cast=()):
    tq, tk = ATTN_Q_TILE, TOKEN_TILE
    q_tiles = seq_len // tq
    k_tiles = seq_len // tk
    assert tok0 % seq_len == 0 and k_tiles >= 2
    q0 = tok0 // tq
    s0 = tok0 // seq_len
    n_lanes = GROUP * tq
    n_steps = n_seq * A_KV_HEADS * q_tiles
    step = lambda b, h, i: (b * A_KV_HEADS + h) * q_tiles + i
    cast_in_specs, cast_out_specs, cast_shapes = [], [], []
    for w, layer in cast:
        _, rows, cols = w.shape
        slab = rows // n_steps
        assert slab * n_steps == rows and slab % BF16_SUBLANES == 0
        cast_in_specs.append(pl.BlockSpec(
            (None, slab, cols), lambda b, h, i, layer=layer: (layer, step(b, h, i), 0)))
        cast_out_specs.append(pl.BlockSpec((None, slab, cols), lambda b, h, i: (0, step(b, h, i), 0)))
        cast_shapes.append(jax.ShapeDtypeStruct((1, rows, cols), BF16))
    out, *casts = pl.pallas_call(
        functools.partial(_global_attn_kernel, n_key_tiles=k_tiles, n_cast=len(cast)),
        grid=(n_seq, A_KV_HEADS, q_tiles),
        in_specs=[
            pl.BlockSpec(memory_space=pltpu.SMEM),
            pl.BlockSpec((GROUP, HEAD_DIM, tq), lambda b, h, i: (h, 0, q0 + b * q_tiles + i)),
            pl.BlockSpec((seq_len, HEAD_DIM), lambda b, h, i: (s0 + b, h)),
            pl.BlockSpec((None, k_tiles, HEAD_DIM, tk), lambda b, h, i: (h, s0 + b, 0, 0)),
            *cast_in_specs,
        ],
        out_specs=[pl.BlockSpec((tq, GROUP * HEAD_DIM), lambda b, h, i: (b * q_tiles + i, h)),
                   *cast_out_specs],
        out_shape=[jax.ShapeDtypeStruct((n_seq * seq_len, A_WIDTH), BF16), *cast_shapes],
        scratch_shapes=[
            pltpu.VMEM((HEAD_DIM, n_lanes), BF16),
            *[pltpu.VMEM((tk, n_lanes), F32)] * ATTN_SLOTS,
            *[pltpu.VMEM((tk, n_lanes), BF16)] * ATTN_SLOTS,
            *[pltpu.VMEM((1, n_lanes), F32)] * ATTN_SLOTS,
            *[pltpu.VMEM((1, n_lanes), F32)] * ATTN_SLOTS,
            pltpu.VMEM((1, n_lanes), F32),
            pltpu.VMEM((SUBLANES, n_lanes), F32),
            pltpu.VMEM((HEAD_DIM, n_lanes), F32),
        ],
        compiler_params=_params("arbitrary", "arbitrary", "arbitrary"),
        name="global_attention",
    )(score_bound, qat, ka, vat, *[w for w, _ in cast])
    return out, casts


def _mix_kernel(*refs, a_starts, x_starts, seq_starts, seq_ends):
    a_refs, refs = refs[:len(a_starts)], refs[len(a_starts):]
    x_refs, refs = refs[:len(x_starts)], refs[len(x_starts):]
    (sink_ref, qt_ref, kp_ref, kc_ref, kn_ref, vtp_ref, vtc_ref, vtn_ref, bias_ref,
     mod_ref, ga_ref, gb_ref, w_ref, gm_ref, o_ref, h_ref, ob_ref) = refs
    i = pl.program_id(0)

    mix_a = (_rms(_read_parts(a_refs, a_starts).astype(F32)) * ga_ref[...]).astype(BF16)
    x_in = _read_parts(x_refs, x_starts)
    gate = mod_ref[MOD_GATE_MIX:MOD_GATE_MIX + 1, :]
    n_chains = B_KV_HEADS * (qt_ref.shape[-1] // Q_BLOCK)
    a_cols = D_MODEL // n_chains

    def project_a(c):
        cs = slice(c * a_cols, (c + 1) * a_cols)
        o_ref[:, cs] = x_in[:, cs] + gate[:, cs] * jnp.dot(
            mix_a, w_ref[:A_WIDTH, cs], preferred_element_type=F32)

    sub_blocks = qt_ref.shape[-1] // Q_BLOCK
    n_lanes = GROUP * Q_BLOCK
    lane_group = lax.broadcasted_iota(jnp.int32, (1, n_lanes), 1) // Q_BLOCK
    neg_inf = jnp.full((WINDOW, n_lanes), -jnp.inf, F32)

    def fold(x, op):
        return functools.reduce(op, [x[r:r + SUBLANES] for r in range(0, x.shape[0], SUBLANES)])

    for h in range(B_KV_HEADS):
        hs = slice(h * HEAD_DIM, (h + 1) * HEAD_DIM)
        k_all = jnp.concatenate([kp_ref[:, hs], kc_ref[:, hs], kn_ref[:, hs]], axis=0)
        vt_all = jnp.concatenate([vtp_ref[h], vtc_ref[h], vtn_ref[h]], axis=1)
        sink = jnp.full((1, n_lanes), sink_ref[h * GROUP] * LOG2E, F32)
        for g in range(1, GROUP):
            sink = jnp.where(lane_group == g, sink_ref[h * GROUP + g] * LOG2E, sink)
        for u in range(sub_blocks):
            project_a(h * sub_blocks + u)
            blk = i * sub_blocks + u
            first = functools.reduce(jnp.logical_or, [blk == s for s in seq_starts])
            last = functools.reduce(jnp.logical_or, [blk == e - 1 for e in seq_ends])
            qs = slice(u * Q_BLOCK, (u + 1) * Q_BLOCK)
            qt = jnp.concatenate([qt_ref[h * GROUP + g, :, qs] for g in range(GROUP)], axis=1)
            kw = k_all[u * Q_BLOCK:u * Q_BLOCK + BAND]
            s = jnp.dot(kw, qt, preferred_element_type=F32) + bias_ref[h]
            s_prev = jnp.where(first, neg_inf, s[:WINDOW])
            s_mid = s[WINDOW:WINDOW + Q_BLOCK]
            s_next = jnp.where(last, neg_inf, s[WINDOW + Q_BLOCK:])
            col_max = functools.reduce(
                jnp.maximum, [fold(v, jnp.maximum) for v in (s_prev, s_mid, s_next)])
            m = jnp.maximum(jnp.max(col_max, axis=0, keepdims=True), sink)
            e = [jnp.exp2(v - m) for v in (s_prev, s_mid, s_next)]
            col_sum = functools.reduce(jnp.add, [fold(v, jnp.add) for v in e])
            den = jnp.sum(col_sum, axis=0, keepdims=True) + jnp.exp2(sink - m)
            p = jnp.concatenate([v.astype(BF16) for v in e], axis=0)
            ot = jnp.dot(vt_all[:, u * Q_BLOCK:u * Q_BLOCK + BAND], p,
                         preferred_element_type=F32) / den
            for g in range(GROUP):
                ob_ref[qs, (h * GROUP + g) * HEAD_DIM:(h * GROUP + g + 1) * HEAD_DIM] = (
                    ot[:, g * Q_BLOCK:(g + 1) * Q_BLOCK].T.astype(BF16))

    mix_b = (_rms(ob_ref[...].astype(F32)) * gb_ref[...]).astype(BF16)
    x = o_ref[...] + gate * jnp.dot(mix_b, w_ref[A_WIDTH:, :], preferred_element_type=F32)
    o_ref[...] = x
    h_ref[...] = (_rms(x) * gm_ref[...] * (1.0 + mod_ref[MOD_SCALE_MLP:MOD_SCALE_MLP + 1, :])
                  + mod_ref[MOD_SHIFT_MLP:MOD_SHIFT_MLP + 1, :]).astype(BF16)


def _mix_tail(out_a_parts, x_parts, qbt, kb, vbt, bias, sink, seq_bounds, mod_rows, out_norm_a,
              out_norm_b, w_out, layer, norm_mlp):
    n_tok = kb.shape[0]
    tm = TOKEN_TILE
    n_tiles = n_tok // tm
    sub = tm // Q_BLOCK
    n_blocks = n_tok // Q_BLOCK
    seg_tiles = n_tiles // mod_rows.shape[0]
    seq_starts = tuple(s // Q_BLOCK for s, _ in seq_bounds)
    seq_ends = tuple(e // Q_BLOCK for _, e in seq_bounds)
    row = lambda i: (i, 0)
    const = lambda i: (0, 0)
    prev_block = lambda i: jnp.maximum(i * sub - 1, 0)
    next_block = lambda i: jnp.minimum(i * sub + sub, n_blocks - 1)
    k_prev = pl.BlockSpec((Q_BLOCK, KV_WIDTH), lambda i: (prev_block(i), 0))
    k_next = pl.BlockSpec((Q_BLOCK, KV_WIDTH), lambda i: (next_block(i), 0))
    vt_cur = pl.BlockSpec((B_KV_HEADS, HEAD_DIM, tm), lambda i: (0, 0, i))
    vt_prev = pl.BlockSpec((B_KV_HEADS, HEAD_DIM, Q_BLOCK), lambda i: (0, 0, prev_block(i)))
    vt_next = pl.BlockSpec((B_KV_HEADS, HEAD_DIM, Q_BLOCK), lambda i: (0, 0, next_block(i)))
    return pl.pallas_call(
        functools.partial(_mix_kernel, a_starts=_part_starts(out_a_parts, tm),
                          x_starts=_part_starts(x_parts, tm), seq_starts=seq_starts,
                          seq_ends=seq_ends),
        grid=(n_tiles,),
        in_specs=[
            *_part_specs(out_a_parts, tm, A_WIDTH),
            *_part_specs(x_parts, tm, D_MODEL),
            pl.BlockSpec(memory_space=pltpu.SMEM),
            pl.BlockSpec((B_Q_HEADS, HEAD_DIM, tm), lambda i: (0, 0, i)),
            k_prev, pl.BlockSpec((tm, KV_WIDTH), row), k_next, vt_prev, vt_cur, vt_next,
            pl.BlockSpec((B_KV_HEADS, BAND, GROUP * Q_BLOCK), lambda i: (0, 0, 0)),
            pl.BlockSpec((None, N_MOD, D_MODEL), lambda i: (i // seg_tiles, 0, 0)),
            pl.BlockSpec((1, A_WIDTH), const),
            pl.BlockSpec((1, B_WIDTH), const),
            pl.BlockSpec((None, MIX_WIDTH, D_MODEL), lambda i: (layer, 0, 0)),
            pl.BlockSpec((1, D_MODEL), const),
        ],
        out_specs=[pl.BlockSpec((tm, D_MODEL), row), pl.BlockSpec((tm, D_MODEL), row)],
        out_shape=[jax.ShapeDtypeStruct((n_tok, D_MODEL), F32),
                   jax.ShapeDtypeStruct((n_tok, D_MODEL), BF16)],
        scratch_shapes=[pltpu.VMEM((tm, B_WIDTH), BF16)],
        compiler_params=_params("arbitrary"),
        name="mix_tail",
    )(*out_a_parts, *x_parts, sink, qbt, kb, kb, kb, vbt, vbt, vbt, bias, mod_rows,
      out_norm_a.reshape(1, A_WIDTH), out_norm_b.reshape(1, B_WIDTH), w_out,
      norm_mlp.reshape(1, D_MODEL))


def _mlp_kernel(x_ref, h_ref, mod_ref, wu_ref, wd_ref, gf_ref, o_ref, *, final_norm):
    k = pl.program_id(1)
    last = pl.num_programs(1) - 1

    def ff_tile(first, final):
        u = jnp.maximum(jnp.dot(h_ref[...], wu_ref[...], preferred_element_type=F32), 0.0)
        part = jnp.dot((u * u).astype(BF16), wd_ref[...], preferred_element_type=F32)
        acc = part if first else o_ref[...] + part
        if final:
            acc = x_ref[...] + mod_ref[MOD_GATE_MLP:MOD_GATE_MLP + 1, :] * acc
            if final_norm:
                acc = _rms(acc) * gf_ref[...]
        o_ref[...] = acc

    pl.when(k == 0)(lambda: ff_tile(True, False))
    pl.when(jnp.logical_and(k > 0, k < last))(lambda: ff_tile(False, False))
    pl.when(k == last)(lambda: ff_tile(False, True))


def _mlp(x, h, mod_rows, w_up, w_down, layer, norm_final, final_norm, tok0=0, n_out=None):
    n_tok = x.shape[0]
    n_out = n_tok if n_out is None else n_out
    tm, tf = MLP_TOKEN_TILE, FF_TILE
    seg = n_tok // mod_rows.shape[0]
    tm = min(tm, seg)
    assert tok0 % tm == 0 and n_out % tm == 0 and D_FF // tf >= 2
    t0, seg_tiles = tok0 // tm, seg // tm
    return pl.pallas_call(
        functools.partial(_mlp_kernel, final_norm=final_norm),
        grid=(n_out // tm, D_FF // tf),
        in_specs=[
            pl.BlockSpec((tm, D_MODEL), lambda i, k: (t0 + i, 0)),
            pl.BlockSpec((tm, D_MODEL), lambda i, k: (t0 + i, 0)),
            pl.BlockSpec((None, N_MOD, D_MODEL), lambda i, k: ((t0 + i) // seg_tiles, 0, 0)),
            pl.BlockSpec((None, D_MODEL, tf), lambda i, k: (layer, 0, k)),
            pl.BlockSpec((None, tf, D_MODEL), lambda i, k: (layer, k, 0)),
            pl.BlockSpec((1, D_MODEL), lambda i, k: (0, 0)),
        ],
        out_specs=pl.BlockSpec((tm, D_MODEL), lambda i, k: (i, 0)),
        out_shape=jax.ShapeDtypeStruct((n_out, D_MODEL), F32),
        compiler_params=_params("arbitrary", "arbitrary"),
        name="mlp",
    )(x, h, mod_rows, w_up, w_down, norm_final.reshape(1, D_MODEL))


def kernel(x_prompt, x_sample, c_prompt, c_sample, w_mod, b_mod, norm_attn, w_in, q_norm, k_norm,
           sink, out_norm_a, out_norm_b, w_out, norm_mlp, w_up, w_down, rel_bias, norm_final):
    n_prompt, t_prompt, _ = x_prompt.shape
    n_sample, t_sample, _ = x_sample.shape
    tok_prompt = n_prompt * t_prompt
    n_tok = tok_prompt + n_sample * t_sample
    seg = math.gcd(t_prompt, t_sample)
    seg_row = ([b for b in range(n_prompt) for _ in range(t_prompt // seg)]
               + [n_prompt + b for b in range(n_sample) for _ in range(t_sample // seg)])
    seq_bounds = ([(b * t_prompt, (b + 1) * t_prompt) for b in range(n_prompt)]
                  + [(tok_prompt + b * t_sample, tok_prompt + (b + 1) * t_sample)
                     for b in range(n_sample)])

    x_parts = (x_prompt.reshape(tok_prompt, D_MODEL), x_sample.reshape(n_tok - tok_prompt, D_MODEL))
    c_rows = jnp.concatenate([c_prompt, c_sample], axis=0)
    c_rows = jnp.pad(c_rows, ((0, MOD_ROWS - c_rows.shape[0]), (0, 0)))
    mod = _modulation(c_rows, w_mod, b_mod)
    mod = mod.reshape(DEPTH, MOD_ROWS, N_MOD, D_MODEL)[:, jnp.asarray(seg_row)]

    rope = _rope_tables(max(t_prompt, t_sample))
    tiles_prompt, tiles_sample = t_prompt // TOKEN_TILE, t_sample // TOKEN_TILE
    n_tiles_prompt = tok_prompt // TOKEN_TILE

    def rope_block(i):
        return jnp.where(i < n_tiles_prompt, i % tiles_prompt, (i - n_tiles_prompt) % tiles_sample)

    bias = _band_bias(rel_bias)
    w_in_b = w_in[:1].astype(BF16)

    for l in range(DEPTH):
        qat, ka, vat, qbt, kb, vbt = _in_projection(x_parts, mod[l], norm_attn[l], w_in_b, 0,
                                                    q_norm[l], k_norm[l], rope, rope_block)
        bound = _score_bound(q_norm[l], k_norm[l])
        cast = [(w_out, l), (w_up, l), (w_down, l)] + ([(w_in, l + 1)] if l + 1 < DEPTH else [])
        out_a_prompt, (w_out_b, w_up_b, w_down_b, *w_in_next) = _global_attention(
            bound, qat, ka, vat, 0, t_prompt, n_prompt, cast)
        out_a_sample, _ = _global_attention(bound, qat, ka, vat, tok_prompt, t_sample, n_sample)
        out_a_parts = (out_a_prompt, out_a_sample)
        x, h = _mix_tail(out_a_parts, x_parts, qbt, kb, vbt, bias, sink[l], seq_bounds, mod[l],
                         out_norm_a[l], out_norm_b[l], w_out_b, 0, norm_mlp[l])
        mlp = functools.partial(_mlp, x, h, mod[l], w_up_b, w_down_b, 0, norm_final)
        if w_in_next:
            w_in_b = w_in_next[0]
        if l < DEPTH - 1:
            x_parts = (mlp(False),)
    y_prompt = mlp(True, 0, tok_prompt).reshape(n_prompt, t_prompt, D_MODEL)
    y_sample = mlp(True, tok_prompt, n_tok - tok_prompt).reshape(n_sample, t_sample, D_MODEL)
    return y_prompt, y_sample
```

```python
import functools
import math

import jax
import jax.numpy as jnp
from jax import lax
from jax.experimental import pallas as pl
from jax.experimental.pallas import tpu as pltpu

D_MODEL = 2048
DEPTH = 4
HEAD_DIM = 128
A_Q_HEADS = 8
A_KV_HEADS = 2
B_Q_HEADS = 8
B_KV_HEADS = 2
GROUP = A_Q_HEADS // A_KV_HEADS
A_WIDTH = A_Q_HEADS * HEAD_DIM
B_WIDTH = B_Q_HEADS * HEAD_DIM
MIX_WIDTH = A_WIDTH + B_WIDTH
KV_WIDTH = A_KV_HEADS * HEAD_DIM
IN_WIDTH = A_WIDTH + 2 * KV_WIDTH + B_WIDTH + 2 * KV_WIDTH
D_FF = 4 * D_MODEL
Q_BLOCK = 128
WINDOW = 128
BAND = Q_BLOCK + 2 * WINDOW
NUM_BUCKETS = 32
MAX_DISTANCE = 128
GRID_W = 64
ROPE_THETA = 10000.0
EPS = 1e-6
SCALE = HEAD_DIM ** -0.5
LOG2E = math.log2(math.e)

OFF_QA = 0
OFF_KA = OFF_QA + A_WIDTH
OFF_VA = OFF_KA + KV_WIDTH
OFF_QB = OFF_VA + KV_WIDTH
OFF_KB = OFF_QB + B_WIDTH
OFF_VB = OFF_KB + KV_WIDTH

MOD_ROWS = 8
N_MOD = 6
MOD_SHIFT_MIX, MOD_SCALE_MIX, MOD_GATE_MIX, MOD_SHIFT_MLP, MOD_SCALE_MLP, MOD_GATE_MLP = range(N_MOD)
TOKEN_TILE = 512
MLP_TOKEN_TILE = 1024
FF_TILE = 1024
ATTN_Q_TILE = 256
ATTN_CHUNKS = 1
ATTN_SLOTS = 3
ATTN_ROUND_UNROLL = 11
SAFE_EXP2_RANGE = 60.0
SUBLANES = 8
BF16_SUBLANES = 16
VMEM_LIMIT = 62 * 1024 * 1024

F32 = jnp.float32
BF16 = jnp.bfloat16


def _params(*semantics):
    return pltpu.CompilerParams(dimension_semantics=semantics, vmem_limit_bytes=VMEM_LIMIT)


def _rms(x):
    return x * lax.rsqrt(jnp.mean(x * x, axis=-1, keepdims=True) + EPS)


def _part_starts(parts, tm):
    starts, start = [], 0
    for p in parts:
        starts.append(start)
        start += p.shape[0] // tm
    return tuple(starts)


def _part_specs(parts, tm, width):
    specs = []
    for p, start in zip(parts, _part_starts(parts, tm)):
        n = p.shape[0] // tm
        specs.append(pl.BlockSpec((tm, width),
                                  lambda i, start=start, n=n: (jnp.clip(i - start, 0, n - 1), 0)))
    return specs


def _read_parts(refs, starts):
    i = pl.program_id(0)
    x = refs[0][...]
    for ref, start in zip(refs[1:], starts[1:]):
        x = jnp.where(i >= start, ref[...], x)
    return x


def _mod_kernel(c_ref, w_ref, b_ref, o_ref):
    c = c_ref[...]
    a = (c * jax.nn.sigmoid(c)).astype(BF16)
    o_ref[...] = jnp.dot(a, w_ref[...].astype(BF16), preferred_element_type=F32) + b_ref[...]


def _modulation(c_rows, w_mod, b_mod):
    tn = D_MODEL
    return pl.pallas_call(
        _mod_kernel,
        grid=(DEPTH, N_MOD * D_MODEL // tn),
        in_specs=[
            pl.BlockSpec((MOD_ROWS, D_MODEL), lambda l, n: (0, 0)),
            pl.BlockSpec((None, D_MODEL, tn), lambda l, n: (l, 0, n)),
            pl.BlockSpec((None, 1, tn), lambda l, n: (l, 0, n)),
        ],
        out_specs=pl.BlockSpec((None, MOD_ROWS, tn), lambda l, n: (l, 0, n)),
        out_shape=jax.ShapeDtypeStruct((DEPTH, MOD_ROWS, N_MOD * D_MODEL), F32),
        compiler_params=_params("arbitrary", "arbitrary"),
        name="modulation",
    )(c_rows, w_mod, b_mod.reshape(DEPTH, 1, N_MOD * D_MODEL))


def _bias_kernel(bucket_ref, rel_bias_ref, o_ref):
    bucket = bucket_ref[...]
    in_band = bucket >= 0
    for h in range(B_Q_HEADS):
        acc = jnp.zeros((BAND, Q_BLOCK), F32)
        for b in range(NUM_BUCKETS):
            acc = jnp.where(bucket == b, rel_bias_ref[b, h], acc)
        g = h % GROUP
        o_ref[h // GROUP, :, g * Q_BLOCK:(g + 1) * Q_BLOCK] = jnp.where(in_band, acc * LOG2E, -jnp.inf)


def _t5_bucket(rel):
    half = NUM_BUCKETS // 2
    max_exact = half // 2
    n = jnp.abs(rel)
    nf = jnp.maximum(n, 1).astype(F32)
    large = max_exact + (jnp.log(nf / max_exact) / math.log(MAX_DISTANCE / max_exact)
                         * (half - max_exact)).astype(jnp.int32)
    large = jnp.minimum(large, half - 1)
    return jnp.where(rel > 0, half, 0) + jnp.where(n < max_exact, n, large)


def _band_bias(rel_bias):
    rel = jnp.arange(BAND)[:, None] - WINDOW - jnp.arange(Q_BLOCK)[None, :]
    bucket = jnp.where(jnp.abs(rel) <= WINDOW, _t5_bucket(rel), -1).astype(jnp.int32)
    return pl.pallas_call(
        _bias_kernel,
        in_specs=[
            pl.BlockSpec(memory_space=pltpu.VMEM),
            pl.BlockSpec(memory_space=pltpu.SMEM),
        ],
        out_specs=pl.BlockSpec(memory_space=pltpu.VMEM),
        out_shape=jax.ShapeDtypeStruct((B_KV_HEADS, BAND, GROUP * Q_BLOCK), F32),
        name="band_bias",
    )(bucket, rel_bias)


def _rope_tables(t_len):
    rows = t_len // GRID_W
    row_ids = jnp.repeat(jnp.arange(rows, dtype=F32), GRID_W)
    col_ids = jnp.tile(jnp.arange(GRID_W, dtype=F32), rows)
    half = HEAD_DIM // 2
    inv_freq = ROPE_THETA ** (-jnp.arange(0, half, 2, dtype=F32) / half)
    ang_r = row_ids[:, None] * inv_freq[None, :]
    ang_c = col_ids[:, None] * inv_freq[None, :]
    ang = jnp.concatenate([ang_r, ang_r, ang_c, ang_c], axis=-1)
    cos, sin = jnp.cos(ang), jnp.sin(ang)
    quarter = (jnp.arange(HEAD_DIM) // (HEAD_DIM // 4))[None, :]
    sin_up = jnp.where(quarter % 2 == 0, -sin, 0.0)
    sin_down = jnp.where(quarter % 2 == 1, sin, 0.0)
    return cos, sin_up, sin_down


def _inproj_kernel(*refs, x_starts):
    x_refs, refs = refs[:len(x_starts)], refs[len(x_starts):]
    (mod_ref, g_ref, w_ref, qn_ref, kn_ref, cos_ref, su_ref, sd_ref,
     qat_ref, ka_ref, vat_ref, qbt_ref, kb_ref, vbt_ref) = refs
    x = _read_parts(x_refs, x_starts)
    h = (_rms(x) * g_ref[...] * (1.0 + mod_ref[MOD_SCALE_MIX:MOD_SCALE_MIX + 1, :])
         + mod_ref[MOD_SHIFT_MIX:MOD_SHIFT_MIX + 1, :]).astype(BF16)

    def proj(lo, width):
        return jnp.dot(h, w_ref[:, lo:lo + width], preferred_element_type=F32)

    cos, s_up, s_down = cos_ref[...], su_ref[...], sd_ref[...]
    quarter = HEAD_DIM // 4

    def norm_rope(z, gain):
        zn = _rms(z) * gain
        return (zn * cos + pltpu.roll(zn, HEAD_DIM - quarter, 1) * s_up
                + pltpu.roll(zn, quarter, 1) * s_down)

    qa = proj(OFF_QA, A_WIDTH)
    for hd in range(A_Q_HEADS):
        sl = slice(hd * HEAD_DIM, (hd + 1) * HEAD_DIM)
        qat_ref[hd] = (norm_rope(qa[:, sl], qn_ref[...]) * (SCALE * LOG2E)).T.astype(BF16)
    ka = proj(OFF_KA, KV_WIDTH)
    for hd in range(A_KV_HEADS):
        sl = slice(hd * HEAD_DIM, (hd + 1) * HEAD_DIM)
        ka_ref[:, sl] = norm_rope(ka[:, sl], kn_ref[...]).astype(BF16)
    va = proj(OFF_VA, KV_WIDTH)
    for hd in range(A_KV_HEADS):
        sl = slice(hd * HEAD_DIM, (hd + 1) * HEAD_DIM)
        vat_ref[hd, 0] = va[:, sl].T.astype(BF16)
    qb = proj(OFF_QB, B_WIDTH)
    for hd in range(B_Q_HEADS):
        sl = slice(hd * HEAD_DIM, (hd + 1) * HEAD_DIM)
        qbt_ref[hd] = (qb[:, sl] * (SCALE * LOG2E)).T.astype(BF16)
    kb_ref[...] = proj(OFF_KB, KV_WIDTH).astype(BF16)
    vb = proj(OFF_VB, KV_WIDTH)
    for hd in range(B_KV_HEADS):
        sl = slice(hd * HEAD_DIM, (hd + 1) * HEAD_DIM)
        vbt_ref[hd] = vb[:, sl].T.astype(BF16)


def _in_projection(x_parts, mod_rows, norm_attn, w_in, layer, q_norm, k_norm, rope, rope_block):
    n_tok = sum(p.shape[0] for p in x_parts)
    tm = TOKEN_TILE
    n_tiles = n_tok // tm
    per_seg = mod_rows.shape[0]
    seg_tiles = n_tiles // per_seg
    cos, s_up, s_down = rope
    row = lambda i: (i, 0)
    const = lambda i: (0, 0)
    rope_spec = pl.BlockSpec((tm, HEAD_DIM), lambda i: (rope_block(i), 0))
    return pl.pallas_call(
        functools.partial(_inproj_kernel, x_starts=_part_starts(x_parts, tm)),
        grid=(n_tiles,),
        in_specs=[
            *_part_specs(x_parts, tm, D_MODEL),
            pl.BlockSpec((None, N_MOD, D_MODEL), lambda i: (i // seg_tiles, 0, 0)),
            pl.BlockSpec((1, D_MODEL), const),
            pl.BlockSpec((None, D_MODEL, IN_WIDTH), lambda i: (layer, 0, 0)),
            pl.BlockSpec((1, HEAD_DIM), const),
            pl.BlockSpec((1, HEAD_DIM), const),
            rope_spec, rope_spec, rope_spec,
        ],
        out_specs=[
            pl.BlockSpec((A_Q_HEADS, HEAD_DIM, tm), lambda i: (0, 0, i)),
            pl.BlockSpec((tm, KV_WIDTH), row),
            pl.BlockSpec((A_KV_HEADS, 1, HEAD_DIM, tm), lambda i: (0, i, 0, 0)),
            pl.BlockSpec((B_Q_HEADS, HEAD_DIM, tm), lambda i: (0, 0, i)),
            pl.BlockSpec((tm, KV_WIDTH), row),
            pl.BlockSpec((B_KV_HEADS, HEAD_DIM, tm), lambda i: (0, 0, i)),
        ],
        out_shape=[
            jax.ShapeDtypeStruct((A_Q_HEADS, HEAD_DIM, n_tok), BF16),
            jax.ShapeDtypeStruct((n_tok, KV_WIDTH), BF16),
            jax.ShapeDtypeStruct((A_KV_HEADS, n_tiles, HEAD_DIM, tm), BF16),
            jax.ShapeDtypeStruct((B_Q_HEADS, HEAD_DIM, n_tok), BF16),
            jax.ShapeDtypeStruct((n_tok, KV_WIDTH), BF16),
            jax.ShapeDtypeStruct((B_KV_HEADS, HEAD_DIM, n_tok), BF16),
        ],
        compiler_params=_params("arbitrary"),
        name="in_projection",
    )(*x_parts, mod_rows, norm_attn.reshape(1, D_MODEL), w_in, q_norm.reshape(1, HEAD_DIM),
      k_norm.reshape(1, HEAD_DIM), cos, s_up, s_down)


def _global_attn_kernel(bound_ref, q_ref, k_ref, vt_ref, *refs, n_key_tiles, n_cast):
    cast_in, o_ref, cast_out = refs[:n_cast], refs[n_cast], refs[n_cast + 1:2 * n_cast + 1]
    qt_ref, scratch = refs[2 * n_cast + 1], refs[2 * n_cast + 2:]
    for w_in_ref, w_out_ref in zip(cast_in, cast_out):
        w_out_ref[...] = w_in_ref[...].astype(BF16)
    tq = q_ref.shape[-1]
    tk = vt_ref.shape[-1]
    n_lanes = GROUP * tq
    rows = tk // ATTN_CHUNKS
    cols = n_lanes // ATTN_CHUNKS
    n = ATTN_SLOTS
    s_bufs, p_bufs, a_bufs, c_bufs = (scratch[i * n:(i + 1) * n] for i in range(4))
    m_ref, l_ref, acc_ref = scratch[4 * n:]
    for g in range(GROUP):
        qt_ref[:, g * tq:(g + 1) * tq] = q_ref[g]
    bounded = bound_ref[0] <= SAFE_EXP2_RANGE
    l_ref[...] = jnp.zeros(l_ref.shape, F32)
    acc_ref[...] = jnp.zeros(acc_ref.shape, F32)

    def fold(x, op):
        return functools.reduce(op, [x[r:r + SUBLANES] for r in range(0, x.shape[0], SUBLANES)])

    def rounds(first, last, one_step, per_round=n):
        n_rounds = (last - first) // per_round

        def round_(i, carry):
            for u in range(per_round):
                one_step(first + per_round * i + u, first + u)
            return carry

        if n_rounds > 1:
            lax.fori_loop(0, n_rounds, round_, 0)
        elif n_rounds == 1:
            round_(0, 0)
        for t in range(first + n_rounds * per_round, last):
            one_step(t, t)

    def bounded_step(qk, pv):
        if qk is not None:
            l_part = l_ref[...]
        for c in range(ATTN_CHUNKS):
            rs = slice(c * rows, (c + 1) * rows)
            cs = slice(c * cols, (c + 1) * cols)
            if qk is not None:
                j, slot = qk
                kc = k_ref[pl.ds(pl.multiple_of(j * tk + c * rows, rows), rows), :]
                p = jnp.exp2(jnp.dot(kc, qt_ref[...], preferred_element_type=F32))
                l_part = l_part + fold(p, jnp.add)
                p_bufs[slot][rs, :] = p.astype(BF16)
            if pv is not None:
                j, slot = pv
                acc_ref[:, cs] += jnp.dot(vt_ref[j], p_bufs[slot][:, cs],
                                          preferred_element_type=F32)
        if qk is not None:
            l_ref[...] = l_part

    @pl.when(bounded)
    def _():
        bounded_step((0, 0), None)
        rounds(0, n_key_tiles - 1,
               lambda t, ts: bounded_step((t + 1, (ts + 1) % n), (t, ts % n)),
               per_round=n * ATTN_ROUND_UNROLL)
        bounded_step(None, (n_key_tiles - 1, (n_key_tiles - 1) % n))

    def step(qk, sm, pv):
        if sm is not None:
            m_prev = m_ref[...]
            m_new = jnp.maximum(m_prev, c_bufs[sm][...])
            alpha = jnp.exp2(m_prev - m_new)
            a_bufs[sm][...] = alpha
            m_ref[...] = m_new
            l_part = alpha * l_ref[...]
        if qk is not None:
            col_max = jnp.full((SUBLANES, n_lanes), -jnp.inf, F32)
        for c in range(ATTN_CHUNKS):
            rs = slice(c * rows, (c + 1) * rows)
            cs = slice(c * cols, (c + 1) * cols)
            if qk is not None:
                j, slot = qk
                kc = k_ref[pl.ds(pl.multiple_of(j * tk + c * rows, rows), rows), :]
                sc = jnp.dot(kc, qt_ref[...], preferred_element_type=F32)
                s_bufs[slot][rs, :] = sc
                col_max = jnp.maximum(col_max, fold(sc, jnp.maximum))
            if sm is not None:
                p = jnp.exp2(s_bufs[sm][rs, :] - m_new)
                l_part = l_part + fold(p, jnp.add)
                p_bufs[sm][rs, :] = p.astype(BF16)
            if pv is not None:
                j, slot = pv
                acc_ref[:, cs] = a_bufs[slot][:, cs] * acc_ref[:, cs] + jnp.dot(
                    vt_ref[j], p_bufs[slot][:, cs], preferred_element_type=F32)
        if qk is not None:
            c_bufs[qk[1]][...] = jnp.max(col_max, axis=0, keepdims=True)
        if sm is not None:
            l_ref[...] = l_part

    @pl.when(jnp.logical_not(bounded))
    def _():
        m_ref[...] = jnp.full(m_ref.shape, -jnp.inf, F32)
        step((0, 0), None, None)
        step((1, 1 % n), 0, None)
        rounds(1, n_key_tiles - 1,
               lambda t, ts: step((t + 1, (ts + 1) % n), ts % n, (t - 1, (ts - 1) % n)))
        step(None, (n_key_tiles - 1) % n, (n_key_tiles - 2, (n_key_tiles - 2) % n))
        step(None, None, (n_key_tiles - 1, (n_key_tiles - 1) % n))

    o = acc_ref[...] / jnp.sum(l_ref[...], axis=0, keepdims=True)
    for g in range(GROUP):
        o_ref[:, g * HEAD_DIM:(g + 1) * HEAD_DIM] = o[:, g * tq:(g + 1) * tq].T.astype(BF16)


def _score_bound(q_norm, k_norm):
    return (HEAD_DIM * SCALE * LOG2E * jnp.max(jnp.abs(q_norm)) * jnp.max(jnp.abs(k_norm))
            ).astype(F32).reshape(1)


def _global_attention(score_bound, qat, ka, vat, tok0, seq_len, n_seq, cast=()):
    tq, tk = ATTN_Q_TILE, TOKEN_TILE
    q_tiles = seq_len // tq
    k_tiles = seq_len // tk
    assert tok0 % seq_len == 0 and k_tiles >= 2
    q0 = tok0 // tq
    s0 = tok0 // seq_len
    n_lanes = GROUP * tq
    n_steps = n_seq * A_KV_HEADS * q_tiles
    step = lambda b, h, i: (b * A_KV_HEADS + h) * q_tiles + i
    cast_in_specs, cast_out_specs, cast_shapes = [], [], []
    for w, layer in cast:
        _, rows, cols = w.shape
        slab = rows // n_steps
        assert slab * n_steps == rows and slab % BF16_SUBLANES == 0
        cast_in_specs.append(pl.BlockSpec(
            (None, slab, cols), lambda b, h, i, layer=layer: (layer, step(b, h, i), 0)))
        cast_out_specs.append(pl.BlockSpec((None, slab, cols), lambda b, h, i: (0, step(b, h, i), 0)))
        cast_shapes.append(jax.ShapeDtypeStruct((1, rows, cols), BF16))
    out, *casts = pl.pallas_call(
        functools.partial(_global_attn_kernel, n_key_tiles=k_tiles, n_cast=len(cast)),
        grid=(n_seq, A_KV_HEADS, q_tiles),
        in_specs=[
            pl.BlockSpec(memory_space=pltpu.SMEM),
            pl.BlockSpec((GROUP, HEAD_DIM, tq), lambda b, h, i: (h, 0, q0 + b * q_tiles + i)),
            pl.BlockSpec((seq_len, HEAD_DIM), lambda b, h, i: (s0 + b, h)),
            pl.BlockSpec((None, k_tiles, HEAD_DIM, tk), lambda b, h, i: (h, s0 + b, 0, 0)),
            *cast_in_specs,
        ],
        out_specs=[pl.BlockSpec((tq, GROUP * HEAD_DIM), lambda b, h, i: (b * q_tiles + i, h)),
                   *cast_out_specs],
        out_shape=[jax.ShapeDtypeStruct((n_seq * seq_len, A_WIDTH), BF16), *cast_shapes],
        scratch_shapes=[
            pltpu.VMEM((HEAD_DIM, n_lanes), BF16),
            *[pltpu.VMEM((tk, n_lanes), F32)] * ATTN_SLOTS,
            *[pltpu.VMEM((tk, n_lanes), BF16)] * ATTN_SLOTS,
            *[pltpu.VMEM((1, n_lanes), F32)] * ATTN_SLOTS,
            *[pltpu.VMEM((1, n_lanes), F32)] * ATTN_SLOTS,
            pltpu.VMEM((1, n_lanes), F32),
            pltpu.VMEM((SUBLANES, n_lanes), F32),
            pltpu.VMEM((HEAD_DIM, n_lanes), F32),
        ],
        compiler_params=_params("arbitrary", "arbitrary", "arbitrary"),
        name="global_attention",
    )(score_bound, qat, ka, vat, *[w for w, _ in cast])
    return out, casts


def _mix_kernel(*refs, a_starts, x_starts, seq_starts, seq_ends):
    a_refs, refs = refs[:len(a_starts)], refs[len(a_starts):]
    x_refs, refs = refs[:len(x_starts)], refs[len(x_starts):]
    (sink_ref, qt_ref, kp_ref, kc_ref, kn_ref, vtp_ref, vtc_ref, vtn_ref, bias_ref,
     mod_ref, ga_ref, gb_ref, w_ref, gm_ref, o_ref, h_ref, ob_ref) = refs
    i = pl.program_id(0)

    mix_a = (_rms(_read_parts(a_refs, a_starts).astype(F32)) * ga_ref[...]).astype(BF16)
    x_in = _read_parts(x_refs, x_starts)
    gate = mod_ref[MOD_GATE_MIX:MOD_GATE_MIX + 1, :]
    n_chains = B_KV_HEADS * (qt_ref.shape[-1] // Q_BLOCK)
    a_cols = D_MODEL // n_chains

    def project_a(c):
        cs = slice(c * a_cols, (c + 1) * a_cols)
        o_ref[:, cs] = x_in[:, cs] + gate[:, cs] * jnp.dot(
            mix_a, w_ref[:A_WIDTH, cs], preferred_element_type=F32)

    sub_blocks = qt_ref.shape[-1] // Q_BLOCK
    n_lanes = GROUP * Q_BLOCK
    lane_group = lax.broadcasted_iota(jnp.int32, (1, n_lanes), 1) // Q_BLOCK
    neg_inf = jnp.full((WINDOW, n_lanes), -jnp.inf, F32)

    def fold(x, op):
        return functools.reduce(op, [x[r:r + SUBLANES] for r in range(0, x.shape[0], SUBLANES)])

    for h in range(B_KV_HEADS):
        hs = slice(h * HEAD_DIM, (h + 1) * HEAD_DIM)
        k_all = jnp.concatenate([kp_ref[:, hs], kc_ref[:, hs], kn_ref[:, hs]], axis=0)
        vt_all = jnp.concatenate([vtp_ref[h], vtc_ref[h], vtn_ref[h]], axis=1)
        sink = jnp.full((1, n_lanes), sink_ref[h * GROUP] * LOG2E, F32)
        for g in range(1, GROUP):
            sink = jnp.where(lane_group == g, sink_ref[h * GROUP + g] * LOG2E, sink)
        for u in range(sub_blocks):
            project_a(h * sub_blocks + u)
            blk = i * sub_blocks + u
            first = functools.reduce(jnp.logical_or, [blk == s for s in seq_starts])
            last = functools.reduce(jnp.logical_or, [blk == e - 1 for e in seq_ends])
            qs = slice(u * Q_BLOCK, (u + 1) * Q_BLOCK)
            qt = jnp.concatenate([qt_ref[h * GROUP + g, :, qs] for g in range(GROUP)], axis=1)
            kw = k_all[u * Q_BLOCK:u * Q_BLOCK + BAND]
            s = jnp.dot(kw, qt, preferred_element_type=F32) + bias_ref[h]
            s_prev = jnp.where(first, neg_inf, s[:WINDOW])
            s_mid = s[WINDOW:WINDOW + Q_BLOCK]
            s_next = jnp.where(last, neg_inf, s[WINDOW + Q_BLOCK:])
            col_max = functools.reduce(
                jnp.maximum, [fold(v, jnp.maximum) for v in (s_prev, s_mid, s_next)])
            m = jnp.maximum(jnp.max(col_max, axis=0, keepdims=True), sink)
            e = [jnp.exp2(v - m) for v in (s_prev, s_mid, s_next)]
            col_sum = functools.reduce(jnp.add, [fold(v, jnp.add) for v in e])
            den = jnp.sum(col_sum, axis=0, keepdims=True) + jnp.exp2(sink - m)
            p = jnp.concatenate([v.astype(BF16) for v in e], axis=0)
            ot = jnp.dot(vt_all[:, u * Q_BLOCK:u * Q_BLOCK + BAND], p,
                         preferred_element_type=F32) / den
            for g in range(GROUP):
                ob_ref[qs, (h * GROUP + g) * HEAD_DIM:(h * GROUP + g + 1) * HEAD_DIM] = (
                    ot[:, g * Q_BLOCK:(g + 1) * Q_BLOCK].T.astype(BF16))

    mix_b = (_rms(ob_ref[...].astype(F32)) * gb_ref[...]).astype(BF16)
    x = o_ref[...] + gate * jnp.dot(mix_b, w_ref[A_WIDTH:, :], preferred_element_type=F32)
    o_ref[...] = x
    h_ref[...] = (_rms(x) * gm_ref[...] * (1.0 + mod_ref[MOD_SCALE_MLP:MOD_SCALE_MLP + 1, :])
                  + mod_ref[MOD_SHIFT_MLP:MOD_SHIFT_MLP + 1, :]).astype(BF16)


def _mix_tail(out_a_parts, x_parts, qbt, kb, vbt, bias, sink, seq_bounds, mod_rows, out_norm_a,
              out_norm_b, w_out, layer, norm_mlp):
    n_tok = kb.shape[0]
    tm = TOKEN_TILE
    n_tiles = n_tok // tm
    sub = tm // Q_BLOCK
    n_blocks = n_tok // Q_BLOCK
    seg_tiles = n_tiles // mod_rows.shape[0]
    seq_starts = tuple(s // Q_BLOCK for s, _ in seq_bounds)
    seq_ends = tuple(e // Q_BLOCK for _, e in seq_bounds)
    row = lambda i: (i, 0)
    const = lambda i: (0, 0)
    prev_block = lambda i: jnp.maximum(i * sub - 1, 0)
    next_block = lambda i: jnp.minimum(i * sub + sub, n_blocks - 1)
    k_prev = pl.BlockSpec((Q_BLOCK, KV_WIDTH), lambda i: (prev_block(i), 0))
    k_next = pl.BlockSpec((Q_BLOCK, KV_WIDTH), lambda i: (next_block(i), 0))
    vt_cur = pl.BlockSpec((B_KV_HEADS, HEAD_DIM, tm), lambda i: (0, 0, i))
    vt_prev = pl.BlockSpec((B_KV_HEADS, HEAD_DIM, Q_BLOCK), lambda i: (0, 0, prev_block(i)))
    vt_next = pl.BlockSpec((B_KV_HEADS, HEAD_DIM, Q_BLOCK), lambda i: (0, 0, next_block(i)))
    return pl.pallas_call(
        functools.partial(_mix_kernel, a_starts=_part_starts(out_a_parts, tm),
                          x_starts=_part_starts(x_parts, tm), seq_starts=seq_starts,
                          seq_ends=seq_ends),
        grid=(n_tiles,),
        in_specs=[
            *_part_specs(out_a_parts, tm, A_WIDTH),
            *_part_specs(x_parts, tm, D_MODEL),
            pl.BlockSpec(memory_space=pltpu.SMEM),
            pl.BlockSpec((B_Q_HEADS, HEAD_DIM, tm), lambda i: (0, 0, i)),
            k_prev, pl.BlockSpec((tm, KV_WIDTH), row), k_next, vt_prev, vt_cur, vt_next,
            pl.BlockSpec((B_KV_HEADS, BAND, GROUP * Q_BLOCK), lambda i: (0, 0, 0)),
            pl.BlockSpec((None, N_MOD, D_MODEL), lambda i: (i // seg_tiles, 0, 0)),
            pl.BlockSpec((1, A_WIDTH), const),
            pl.BlockSpec((1, B_WIDTH), const),
            pl.BlockSpec((None, MIX_WIDTH, D_MODEL), lambda i: (layer, 0, 0)),
            pl.BlockSpec((1, D_MODEL), const),
        ],
        out_specs=[pl.BlockSpec((tm, D_MODEL), row), pl.BlockSpec((tm, D_MODEL), row)],
        out_shape=[jax.ShapeDtypeStruct((n_tok, D_MODEL), F32),
                   jax.ShapeDtypeStruct((n_tok, D_MODEL), BF16)],
        scratch_shapes=[pltpu.VMEM((tm, B_WIDTH), BF16)],
        compiler_params=_params("arbitrary"),
        name="mix_tail",
    )(*out_a_parts, *x_parts, sink, qbt, kb, kb, kb, vbt, vbt, vbt, bias, mod_rows,
      out_norm_a.reshape(1, A_WIDTH), out_norm_b.reshape(1, B_WIDTH), w_out,
      norm_mlp.reshape(1, D_MODEL))


def _mlp_kernel(x_ref, h_ref, mod_ref, wu_ref, wd_ref, gf_ref, o_ref, *, final_norm):
    k = pl.program_id(1)

    @pl.when(k == 0)
    def _():
        o_ref[...] = jnp.zeros(o_ref.shape, F32)

    u = jnp.maximum(jnp.dot(h_ref[...], wu_ref[...], preferred_element_type=F32), 0.0)
    o_ref[...] += jnp.dot((u * u).astype(BF16), wd_ref[...], preferred_element_type=F32)

    @pl.when(k == pl.num_programs(1) - 1)
    def _():
        y = x_ref[...] + mod_ref[MOD_GATE_MLP:MOD_GATE_MLP + 1, :] * o_ref[...]
        if final_norm:
            y = _rms(y) * gf_ref[...]
        o_ref[...] = y


def _mlp(x, h, mod_rows, w_up, w_down, layer, norm_final, final_norm, tok0=0, n_out=None):
    n_tok = x.shape[0]
    n_out = n_tok if n_out is None else n_out
    tm, tf = MLP_TOKEN_TILE, FF_TILE
    seg = n_tok // mod_rows.shape[0]
    tm = min(tm, seg)
    assert tok0 % tm == 0 and n_out % tm == 0
    t0, seg_tiles = tok0 // tm, seg // tm
    return pl.pallas_call(
        functools.partial(_mlp_kernel, final_norm=final_norm),
        grid=(n_out // tm, D_FF // tf),
        in_specs=[
            pl.BlockSpec((tm, D_MODEL), lambda i, k: (t0 + i, 0), pipeline_mode=pl.Buffered(1)),
            pl.BlockSpec((tm, D_MODEL), lambda i, k: (t0 + i, 0)),
            pl.BlockSpec((None, N_MOD, D_MODEL), lambda i, k: ((t0 + i) // seg_tiles, 0, 0)),
            pl.BlockSpec((None, D_MODEL, tf), lambda i, k: (layer, 0, k)),
            pl.BlockSpec((None, tf, D_MODEL), lambda i, k: (layer, k, 0)),
            pl.BlockSpec((1, D_MODEL), lambda i, k: (0, 0)),
        ],
        out_specs=pl.BlockSpec((tm, D_MODEL), lambda i, k: (i, 0)),
        out_shape=jax.ShapeDtypeStruct((n_out, D_MODEL), F32),
        compiler_params=_params("arbitrary", "arbitrary"),
        name="mlp",
    )(x, h, mod_rows, w_up, w_down, norm_final.reshape(1, D_MODEL))


def kernel(x_prompt, x_sample, c_prompt, c_sample, w_mod, b_mod, norm_attn, w_in, q_norm, k_norm,
           sink, out_norm_a, out_norm_b, w_out, norm_mlp, w_up, w_down, rel_bias, norm_final):
    n_prompt, t_prompt, _ = x_prompt.shape
    n_sample, t_sample, _ = x_sample.shape
    tok_prompt = n_prompt * t_prompt
    n_tok = tok_prompt + n_sample * t_sample
    seg = math.gcd(t_prompt, t_sample)
    seg_row = ([b for b in range(n_prompt) for _ in range(t_prompt // seg)]
               + [n_prompt + b for b in range(n_sample) for _ in range(t_sample // seg)])
    seq_bounds = ([(b * t_prompt, (b + 1) * t_prompt) for b in range(n_prompt)]
                  + [(tok_prompt + b * t_sample, tok_prompt + (b + 1) * t_sample)
                     for b in range(n_sample)])

    x_parts = (x_prompt.reshape(tok_prompt, D_MODEL), x_sample.reshape(n_tok - tok_prompt, D_MODEL))
    c_rows = jnp.concatenate([c_prompt, c_sample], axis=0)
    c_rows = jnp.pad(c_rows, ((0, MOD_ROWS - c_rows.shape[0]), (0, 0)))
    mod = _modulation(c_rows, w_mod, b_mod)
    mod = mod.reshape(DEPTH, MOD_ROWS, N_MOD, D_MODEL)[:, jnp.asarray(seg_row)]

    rope = _rope_tables(max(t_prompt, t_sample))
    tiles_prompt, tiles_sample = t_prompt // TOKEN_TILE, t_sample // TOKEN_TILE
    n_tiles_prompt = tok_prompt // TOKEN_TILE

    def rope_block(i):
        return jnp.where(i < n_tiles_prompt, i % tiles_prompt, (i - n_tiles_prompt) % tiles_sample)

    bias = _band_bias(rel_bias)
    w_in_b = w_in[:1].astype(BF16)

    for l in range(DEPTH):
        qat, ka, vat, qbt, kb, vbt = _in_projection(x_parts, mod[l], norm_attn[l], w_in_b, 0,
                                                    q_norm[l], k_norm[l], rope, rope_block)
        bound = _score_bound(q_norm[l], k_norm[l])
        cast = [(w_out, l), (w_up, l), (w_down, l)] + ([(w_in, l + 1)] if l + 1 < DEPTH else [])
        out_a_prompt, (w_out_b, w_up_b, w_down_b, *w_in_next) = _global_attention(
            bound, qat, ka, vat, 0, t_prompt, n_prompt, cast)
        out_a_sample, _ = _global_attention(bound, qat, ka, vat, tok_prompt, t_sample, n_sample)
        out_a_parts = (out_a_prompt, out_a_sample)
        x, h = _mix_tail(out_a_parts, x_parts, qbt, kb, vbt, bias, sink[l], seq_bounds, mod[l],
                         out_norm_a[l], out_norm_b[l], w_out_b, 0, norm_mlp[l])
        mlp = functools.partial(_mlp, x, h, mod[l], w_up_b, w_down_b, 0, norm_final)
        if w_in_next:
            w_in_b = w_in_next[0]
        if l < DEPTH - 1:
            x_parts = (mlp(False),)
    y_prompt = mlp(True, 0, tok_prompt).reshape(n_prompt, t_prompt, D_MODEL)
    y_sample = mlp(True, tok_prompt, n_tok - tok_prompt).reshape(n_sample, t_sample, D_MODEL)
    return y_prompt, y_sample
```
